```python
import math
import jax
import jax.numpy as jnp
from jax import lax
import numpy as np


D_MODEL = 1024
BATCH = 8
SEQ = 2048
DEPTH = 4
DEC_BATCH = 16
DEC_SEQ = 4096
PAST_LEN = 128

PLE_DIM = 256
HG_WIDTH = D_MODEL // 2
HG_HEADS = 4
HG_DK = HG_WIDTH // HG_HEADS
HG_DV = HG_WIDTH // HG_HEADS
HG_CHUNK = 64
LB_FLOOR = 1e-30
DA_WIDTH = D_MODEL // 2
DA_HEADS = 4
DA_DV = DA_WIDTH // DA_HEADS
DA_DH = DA_DV // 2
Q_BLOCK = 128
D_FF = 2816
CONV_W = 3
ROPE_THETA = 10000.0
EPS = 1e-6
IN_COLS = 5 * HG_WIDTH + 3 * DA_WIDTH + 2 * D_MODEL

kernel_name = "hybrid_hgrn2_diffattn_encoder"


def rms_norm(x, w):
    xf = x.astype(jnp.float32)
    y = xf * lax.rsqrt(jnp.mean(xf * xf, axis=-1, keepdims=True) + EPS)
    return (y * w.astype(jnp.float32)).astype(x.dtype)


def rms_unit(x):
    xf = x.astype(jnp.float32)
    return (xf * lax.rsqrt(jnp.mean(xf * xf, axis=-1, keepdims=True) + EPS)).astype(x.dtype)


def rope_tables(L, dtype):
    inv = 1.0 / (ROPE_THETA ** (jnp.arange(0, DA_DH, 2, dtype=jnp.float32) / DA_DH))
    ang = jnp.arange(L, dtype=jnp.float32)[:, None] * inv[None, :]
    ang = jnp.concatenate([ang, ang], axis=-1)
    return jnp.cos(ang).astype(dtype), jnp.sin(ang).astype(dtype)


def apply_rope(x, cos, sin):
    x1, x2 = jnp.split(x, 2, axis=-1)
    return x * cos + jnp.concatenate([-x2, x1], axis=-1) * sin


def gla_chunk_scan(q, k, v, log_f):
    B, H, L, dk = q.shape
    dv = v.shape[-1]
    n = L // HG_CHUNK

    def to_chunks(a):
        return jnp.moveaxis(a.reshape(B, H, n, HG_CHUNK, a.shape[-1]), 2, 0)

    causal = jnp.tril(jnp.ones((HG_CHUNK, HG_CHUNK), dtype=bool))[:, :, None]

    def step(S, inp):
        qc, kc, vc, gc = inp
        b = jnp.cumsum(gc, axis=-2)
        diff = b[..., :, None, :] - b[..., None, :, :]
        decay = jnp.where(causal, jnp.exp(jnp.where(causal, diff, 0.0)), 0.0)
        A = jnp.einsum('bhtk,bhsk,bhtsk->bhts', qc, kc, decay)
        o = (jnp.einsum('bhts,bhsv->bhtv', A, vc)
             + jnp.einsum('bhtk,bhkv->bhtv', qc * jnp.exp(b), S))
        b_last = b[..., -1:, :]
        S = (jnp.exp(b_last[..., 0, :])[..., None] * S
             + jnp.einsum('bhsk,bhsv->bhkv', kc * jnp.exp(b_last - b), vc))
        return S, o

    S0 = jnp.zeros((B, H, dk, dv), jnp.float32)
    _, o = lax.scan(step, S0, (to_chunks(q), to_chunks(k), to_chunks(v), to_chunks(log_f)))
    return jnp.moveaxis(o, 0, 2).reshape(B, H, L, dv)


def hgrn2_mixer(q_raw, f_fw_raw, f_bw_raw, i_raw, g_raw, lb, gnorm_w):
    B, L, _ = q_raw.shape
    dt = q_raw.dtype

    def heads(a):
        return a.reshape(B, L, HG_HEADS, -1).transpose(0, 2, 1, 3)

    q = heads(jax.nn.silu(q_raw.astype(jnp.float32))) * (HG_DK ** -0.5)
    v = heads(i_raw.astype(jnp.float32))

    def gates(f_raw, lb_d):
        z = f_raw.astype(jnp.float32)
        log_f = jnp.logaddexp(jnp.log(jnp.maximum(lb_d, LB_FLOOR)),
                              jnp.log1p(-lb_d) + jax.nn.log_sigmoid(z))
        k = (1.0 - lb_d) * jax.nn.sigmoid(-z)
        return heads(k), heads(log_f)

    k_fw, g_fw = gates(f_fw_raw, lb[0])
    k_bw, g_bw = gates(f_bw_raw, lb[1])
    o_fw = gla_chunk_scan(q, k_fw, v, g_fw)
    o_bw = jnp.flip(gla_chunk_scan(jnp.flip(q, 2), jnp.flip(k_bw, 2), jnp.flip(v, 2), jnp.flip(g_bw, 2)), 2)
    o = (o_fw + o_bw).transpose(0, 2, 1, 3)
    gate = jax.nn.silu(g_raw.astype(jnp.float32).reshape(B, L, HG_HEADS, HG_DV))
    o = rms_norm(o, gnorm_w) * gate
    return o.reshape(B, L, HG_WIDTH).astype(dt)


def diff_attention(q_raw, k_raw, v_raw, lam_params, lambda_init, subln_w, cos, sin):
    B, L, _ = q_raw.shape
    q = q_raw.reshape(B, L, DA_HEADS, 2, DA_DH).transpose(0, 2, 3, 1, 4)
    k = k_raw.reshape(B, L, DA_HEADS, 2, DA_DH).transpose(0, 2, 3, 1, 4)
    v = v_raw.reshape(B, L, DA_HEADS, DA_DV).transpose(0, 2, 1, 3)
    q = apply_rope(q, cos, sin) * (DA_DH ** -0.5)
    k = apply_rope(k, cos, sin)
    lp = lam_params.astype(jnp.float32)
    lam = jnp.exp(jnp.sum(lp[0] * lp[1])) - jnp.exp(jnp.sum(lp[2] * lp[3])) + lambda_init
    nb = L // Q_BLOCK
    qb = jnp.moveaxis(q.reshape(B, DA_HEADS, 2, nb, Q_BLOCK, DA_DH), 3, 0)

    def block(q_blk):
        s = jnp.einsum('bhmqd,bhmkd->bhmqk', q_blk, k).astype(jnp.float32)
        p = jax.nn.softmax(s, axis=-1)
        a = p[:, :, 0] - lam * p[:, :, 1]
        return jnp.einsum('bhqk,bhkv->bhqv', a.astype(v.dtype), v)

    o = lax.map(block, qb)
    o = jnp.moveaxis(o, 0, 2).reshape(B, DA_HEADS, L, DA_DV)
    o = rms_norm(o, subln_w) * (1.0 - lambda_init)
    return o.transpose(0, 2, 1, 3).reshape(B, L, DA_WIDTH)


def conv_ffn(h, w_up, conv_w, conv_b, w_down):
    u = h @ w_up
    up = jnp.pad(u, ((0, 0), (1, 1), (0, 0)))
    c = up[:, :-2] * conv_w[0] + up[:, 1:-1] * conv_w[1] + up[:, 2:] * conv_w[2] + conv_b
    gate, val = jnp.split(c, 2, axis=-1)
    return (jax.nn.gelu(gate, approximate=True) * val) @ w_down


def _trunk(x, p, w_in, hgrn_lb_logits, hgrn_gnorm, diff_lambda, diff_subln,
           w_branch_a, w_branch_b, w_out, norm_mix_pre, norm_mix_post,
           w_up, conv_w, conv_b, w_down, norm_ffn_pre, norm_ffn_post,
           w_ple, w_ple_gate, norm_ple):
    L = x.shape[1]
    cos, sin = rope_tables(L, x.dtype)
    lb = jax.nn.softmax(hgrn_lb_logits.astype(jnp.float32), axis=0)
    lb = jnp.cumsum(lb, axis=0) - lb[0]
    splits = ([HG_WIDTH * i for i in range(1, 6)]
              + [5 * HG_WIDTH + DA_WIDTH * i for i in range(1, 4)]
              + [5 * HG_WIDTH + 3 * DA_WIDTH + D_MODEL])
    for l in range(DEPTH):
        lambda_init = 0.8 - 0.6 * math.exp(-0.3 * l)
        h = rms_norm(x, norm_mix_pre[l])
        z = h @ w_in[l]
        q_h, f_fw, f_bw, i_h, g_h, q_d, k_d, v_d, gate_a, gate_b = jnp.split(z, splits, axis=-1)
        a = hgrn2_mixer(q_h, f_fw, f_bw, i_h, g_h, lb[l], hgrn_gnorm[l])
        b = diff_attention(q_d, k_d, v_d, diff_lambda[l], lambda_init, diff_subln[l], cos, sin)
        m = jax.nn.sigmoid(gate_a) * (a @ w_branch_a[l]) + jax.nn.sigmoid(gate_b) * (b @ w_branch_b[l])
        x = x + rms_norm(m @ w_out[l], norm_mix_post[l])
        f = conv_ffn(rms_norm(x, norm_ffn_pre[l]), w_up[l], conv_w[l], conv_b[l], w_down[l])
        x = x + rms_norm(f, norm_ffn_post[l])
        e = p[l] @ w_ple[l]
        g = jax.nn.sigmoid(rms_unit(x) @ w_ple_gate[l])
        x = x + rms_norm(g * e, norm_ple[l])
    return x


def setup_inputs(seed: int = 0) -> dict:
    key = jax.random.key(seed)
    ks = jax.random.split(key, 24)

    def nrm(k, shape, s):
        return s * jax.random.normal(k, shape, jnp.float32)

    def gain(k, shape):
        return 1.0 + 0.05 * jax.random.normal(k, shape, jnp.float32)

    return {
        "x_prompt": nrm(ks[0], (BATCH, SEQ, D_MODEL), 1.0),
        "x_sample": nrm(ks[1], (DEC_BATCH, DEC_SEQ, D_MODEL), 1.0),
        "p_prompt": nrm(ks[2], (DEPTH, BATCH, SEQ, PLE_DIM), 1.0),
        "p_sample": nrm(ks[3], (DEPTH, DEC_BATCH, DEC_SEQ, PLE_DIM), 1.0),
        "w_in": nrm(ks[4], (DEPTH, D_MODEL, IN_COLS), D_MODEL ** -0.5),
        "hgrn_lb_logits": nrm(ks[5], (DEPTH, 2, HG_WIDTH), 0.1),
        "hgrn_gnorm": gain(ks[6], (DEPTH, HG_DV)),
        "diff_lambda": nrm(ks[7], (DEPTH, 4, DA_DH), 0.1),
        "diff_subln": gain(ks[8], (DEPTH, DA_DV)),
        "w_branch_a": nrm(ks[9], (DEPTH, HG_WIDTH, D_MODEL), HG_WIDTH ** -0.5),
        "w_branch_b": nrm(ks[10], (DEPTH, DA_WIDTH, D_MODEL), DA_WIDTH ** -0.5),
        "w_out": nrm(ks[11], (DEPTH, D_MODEL, D_MODEL), D_MODEL ** -0.5),
        "norm_mix_pre": gain(ks[12], (DEPTH, D_MODEL)),
        "norm_mix_post": gain(ks[13], (DEPTH, D_MODEL)),
        "w_up": nrm(ks[14], (DEPTH, D_MODEL, 2 * D_FF), D_MODEL ** -0.5),
        "conv_w": nrm(ks[15], (DEPTH, CONV_W, 2 * D_FF), CONV_W ** -0.5),
        "conv_b": nrm(ks[16], (DEPTH, 2 * D_FF), 0.01),
        "w_down": nrm(ks[17], (DEPTH, D_FF, D_MODEL), D_FF ** -0.5),
        "norm_ffn_pre": gain(ks[18], (DEPTH, D_MODEL)),
        "norm_ffn_post": gain(ks[19], (DEPTH, D_MODEL)),
        "w_ple": nrm(ks[20], (DEPTH, PLE_DIM, D_MODEL), PLE_DIM ** -0.5),
        "w_ple_gate": nrm(ks[21], (DEPTH, D_MODEL, D_MODEL), D_MODEL ** -0.5),
        "norm_ple": gain(ks[22], (DEPTH, D_MODEL)),
    }


def reference(x_prompt, x_sample, p_prompt, p_sample, w_in, hgrn_lb_logits, hgrn_gnorm,
              diff_lambda, diff_subln, w_branch_a, w_branch_b, w_out, norm_mix_pre,
              norm_mix_post, w_up, conv_w, conv_b, w_down, norm_ffn_pre, norm_ffn_post,
              w_ple, w_ple_gate, norm_ple):
    y_prompt = _trunk(x_prompt, p_prompt, w_in, hgrn_lb_logits, hgrn_gnorm, diff_lambda,
                      diff_subln, w_branch_a, w_branch_b, w_out, norm_mix_pre, norm_mix_post,
                      w_up, conv_w, conv_b, w_down, norm_ffn_pre, norm_ffn_post,
                      w_ple, w_ple_gate, norm_ple)
    y_sample = _trunk(x_sample, p_sample, w_in, hgrn_lb_logits, hgrn_gnorm, diff_lambda,
                      diff_subln, w_branch_a, w_branch_b, w_out, norm_mix_pre, norm_mix_post,
                      w_up, conv_w, conv_b, w_down, norm_ffn_pre, norm_ffn_post,
                      w_ple, w_ple_gate, norm_ple)
    return (y_prompt, y_sample)
```

```python
import functools
import math

import jax
import jax.numpy as jnp
from jax import lax
from jax.experimental import pallas as pl
from jax.experimental.pallas import tpu as pltpu

D_MODEL = 1024
DEPTH = 4
PLE_DIM = 256
HG_WIDTH = D_MODEL // 2
HG_HEADS = 4
HG_DK = HG_WIDTH // HG_HEADS
LB_FLOOR = 1e-30
DA_WIDTH = D_MODEL // 2
DA_HEADS = 4
DA_DV = DA_WIDTH // DA_HEADS
DA_DH = DA_DV // 2
D_FF = 2816
ROPE_THETA = 10000.0
EPS = 1e-6
IN_COLS = 5 * HG_WIDTH + 3 * DA_WIDTH + 2 * D_MODEL

LANES = 128
SUBLANES = 8
HG_CHUNK = SUBLANES * SUBLANES
VMEM_LIMIT = 56 * 1024 * 1024

BF16 = jnp.bfloat16
F32 = jnp.float32


def _dot(a, b):
    return jnp.dot(a, b, preferred_element_type=F32)


def _dot_nt(a, b):
    return lax.dot_general(a, b, (((1,), (1,)), ((), ())), preferred_element_type=F32)


def _dot_tn(a, b):
    return lax.dot_general(a, b, (((0,), (0,)), ((), ())), preferred_element_type=F32)


def _rms(x, w=None):
    y = x * lax.rsqrt(jnp.mean(x * x, axis=-1, keepdims=True) + EPS)
    return y if w is None else y * w


def _sigmoid(x):
    return 1.0 / (1.0 + jnp.exp(-x))


def _const_spec(block_shape, index_map):
    return pl.BlockSpec(block_shape, index_map, pipeline_mode=pl.Buffered(1))


def _rope(z, cos, sin_signed, first_half):
    rot = jnp.where(first_half, pltpu.roll(z, 96, axis=1), pltpu.roll(z, 32, axis=1))
    return z * cos + rot * sin_signed


def _in_proj_kernel(x_ref, nw_ref, w_ref, cos_ref, sin_ref,
                    hg_ref, q_ref, k_ref, v_ref, gate_ref):
    h = _rms(x_ref[...], nw_ref[...]).astype(BF16)
    cw = HG_WIDTH
    for c in range(5):
        hg_ref[:, c * cw:(c + 1) * cw] = _dot(h, w_ref[:, c * cw:(c + 1) * cw])
    cos = cos_ref[...]
    sin = sin_ref[...]
    lane = lax.broadcasted_iota(jnp.int32, cos.shape, 1)
    first_half = (lane & (DA_DH - 1)) < (DA_DH // 2)
    base = 5 * HG_WIDTH
    zq = _dot(h, w_ref[:, base:base + DA_WIDTH])
    zk = _dot(h, w_ref[:, base + DA_WIDTH:base + 2 * DA_WIDTH])
    for hd in range(DA_HEADS):
        sl = slice(hd * LANES, (hd + 1) * LANES)
        q_ref[:, sl] = (_rope(zq[:, sl], cos, sin, first_half) * (DA_DH ** -0.5)).astype(BF16)
        k_ref[:, sl] = _rope(zk[:, sl], cos, sin, first_half).astype(BF16)
    v_ref[...] = _dot(h, w_ref[:, base + 2 * DA_WIDTH:base + 3 * DA_WIDTH]).astype(BF16)
    gbase = base + 3 * DA_WIDTH
    for c in range(4):
        zg = _dot(h, w_ref[:, gbase + c * cw:gbase + (c + 1) * cw])
        gate_ref[:, c * cw:(c + 1) * cw] = _sigmoid(zg).astype(BF16)


def _in_proj(x2d, norm_w, w_in, cos, sin, layer, seq, tm):
    T = x2d.shape[0]
    npos = seq // tm
    return pl.pallas_call(
        _in_proj_kernel,
        grid=(T // tm,),
        in_specs=[
            pl.BlockSpec((tm, D_MODEL), lambda i: (i, 0)),
            _const_spec((None, 1, D_MODEL), lambda i: (layer, 0, 0)),
            _const_spec((None, D_MODEL, IN_COLS), lambda i: (layer, 0, 0)),
            pl.BlockSpec((tm, LANES), lambda i: (i % npos, 0)),
            pl.BlockSpec((tm, LANES), lambda i: (i % npos, 0)),
        ],
        out_specs=[
            pl.BlockSpec((tm, 5 * HG_WIDTH), lambda i: (i, 0)),
            pl.BlockSpec((tm, DA_WIDTH), lambda i: (i, 0)),
            pl.BlockSpec((tm, DA_WIDTH), lambda i: (i, 0)),
            pl.BlockSpec((tm, DA_WIDTH), lambda i: (i, 0)),
            pl.BlockSpec((tm, 2 * D_MODEL), lambda i: (i, 0)),
        ],
        out_shape=[
            jax.ShapeDtypeStruct((T, 5 * HG_WIDTH), F32),
            jax.ShapeDtypeStruct((T, DA_WIDTH), BF16),
            jax.ShapeDtypeStruct((T, DA_WIDTH), BF16),
            jax.ShapeDtypeStruct((T, DA_WIDTH), BF16),
            jax.ShapeDtypeStruct((T, 2 * D_MODEL), BF16),
        ],
        compiler_params=pltpu.CompilerParams(
            dimension_semantics=("parallel",), vmem_limit_bytes=VMEM_LIMIT),
        name="in_proj",
    )(x2d, norm_w, w_in, cos, sin)


def _hgrn_consts():
    blk = jnp.arange(SUBLANES)[:, None]
    cblk = (jnp.arange(HG_CHUNK) % SUBLANES)[None, :]
    masks = []
    for rev in (False, True):
        for m in (1, 2, 4):
            same = (blk // (2 * m)) == (cblk // (2 * m))
            r_hi = (blk % (2 * m)) >= m
            c_hi = (cblk % (2 * m)) >= m
            mk = same & (r_hi & ~c_hi if not rev else ~r_hi & c_hi)
            masks.append(mk)
    masks.append(blk == cblk)
    masks = jnp.stack(masks).astype(F32)
    grp = (jnp.arange(SUBLANES * LANES) // LANES)[:, None]
    cgrp = (jnp.arange(HG_CHUNK) // SUBLANES)[None, :]
    wsum = (grp == cgrp).astype(BF16)
    return masks, wsum


def _gather_rows(x, src, row):
    out = None
    for s in sorted({(i - src[i]) % SUBLANES for i in range(SUBLANES)}):
        rolled = x if s == 0 else pltpu.roll(x, s, axis=0)
        if out is None:
            out = rolled
        else:
            sel = functools.reduce(
                jnp.logical_or,
                [row == i for i in range(SUBLANES) if (i - src[i]) % SUBLANES == s])
            out = jnp.where(sel, rolled, out)
    return out


def _hgrn_chunk(q, kk, v, lg, st_ref, masks_ref, wsum_ref, rev, row):
    n = SUBLANES
    order = list(range(n - 1, -1, -1)) if rev else list(range(n))
    w = [None] * n
    acc = None
    for j in order:
        acc = lg[j] if acc is None else acc + lg[j]
        w[j] = acc
    tot = acc
    incl = tot
    for s in (1, 2, 4):
        if not rev:
            incl = incl + jnp.where(row >= s, pltpu.roll(incl, s, axis=0), 0.0)
        else:
            incl = incl + jnp.where(row < n - s, pltpu.roll(incl, n - s, axis=0), 0.0)
    if not rev:
        excl = jnp.where(row >= 1, pltpu.roll(incl, 1, axis=0), 0.0)
        chunk_tot = jnp.broadcast_to(incl[n - 1:n, :], incl.shape)
    else:
        excl = jnp.where(row < n - 1, pltpu.roll(incl, n - 1, axis=0), 0.0)
        chunk_tot = jnp.broadcast_to(incl[0:1, :], incl.shape)
    b = [w[j] + excl for j in range(n)]

    st = st_ref[...]
    qe = jnp.concatenate([q[j] * jnp.exp(b[j]) for j in range(n)], axis=0)
    o = _dot_nt(qe.astype(BF16), st.astype(BF16))

    a = None
    for li, m in enumerate((1, 2, 4)):
        if not rev:
            src = [(i // (2 * m)) * 2 * m + m - 1 for i in range(n)]
        else:
            src = [(i // (2 * m)) * 2 * m + m for i in range(n)]
        y = excl - _gather_rows(incl, src, row)
        e = [jnp.exp(-jnp.abs(w[j] + y)) for j in range(n)]
        qm = jnp.concatenate([q[j] * e[j] for j in range(n)], axis=0).astype(BF16)
        km = jnp.concatenate([kk[j] * e[j] for j in range(n)], axis=0).astype(BF16)
        am = _dot_nt(qm, km)
        mk = masks_ref[(3 if rev else 0) + li]
        am = am.reshape(n, n, HG_CHUNK) * mk[None]
        a = am if a is None else a + am

    zero = jnp.zeros_like(q[0])
    rows_p = []
    for j in range(n):
        blocks = []
        for jp in range(n):
            earlier = (jp <= j) if not rev else (jp >= j)
            if not earlier:
                blocks.append(zero)
            elif jp == j:
                blocks.append(q[j] * kk[j])
            else:
                blocks.append(q[j] * kk[jp] * jnp.exp(w[j] - w[jp]))
        rows_p.append(jnp.concatenate(blocks, axis=1))
    pbig = jnp.concatenate(rows_p, axis=0).astype(BF16)
    ad = _dot(pbig, wsum_ref[...])
    a = a + ad.reshape(n, n, HG_CHUNK) * masks_ref[6][None]
    vcat = jnp.concatenate(v, axis=0).astype(BF16)
    o = o + _dot(a.reshape(HG_CHUNK, HG_CHUNK).astype(BF16), vcat)

    kd = jnp.concatenate([kk[j] * jnp.exp(chunk_tot - b[j]) for j in range(n)], axis=0)
    st_ref[...] = jnp.exp(chunk_tot[0:1, :]) * st + _dot_tn(vcat, kd.astype(BF16))
    return [o[j * n:(j + 1) * n, :] for j in range(n)]


def _hgrn_kernel(q_ref, ff_ref, fb_ref, i_ref, g_ref, lbl_ref, gn_ref, masks_ref, wsum_ref,
                 out_ref, ofw_ref, obw_ref, sfw_ref, sbw_ref, *, layer, seq):
    n = SUBLANES
    nchunks = seq // HG_CHUNK
    row = lax.broadcasted_iota(jnp.int32, (n, HG_DK), 0)

    def lower_bound(d):
        rows = [lbl_ref[2 * l + d:2 * l + d + 1, :] for l in range(DEPTH)]
        mx = functools.reduce(jnp.maximum, rows)
        ex = [jnp.exp(r - mx) for r in rows]
        den = functools.reduce(lambda a_, b_: a_ + b_, ex)
        sm = [e / den for e in ex]
        cs = sm[0]
        for l in range(1, layer + 1):
            cs = cs + sm[l]
        return cs - sm[0]

    lb = [lower_bound(0), lower_bound(1)]
    log_lb = [jnp.log(jnp.maximum(x, LB_FLOOR)) for x in lb]
    log1m_lb = [jnp.log1p(-x) for x in lb]
    one_m_lb = [1.0 - x for x in lb]

    sfw_ref[...] = jnp.zeros_like(sfw_ref)
    sbw_ref[...] = jnp.zeros_like(sbw_ref)

    def load(ref, base, j):
        return ref[pl.ds(base + j, n, stride=n), :]

    def gates(z, d):
        t = jnp.exp(-jnp.abs(z))
        r = 1.0 / (1.0 + t)
        sig_neg = jnp.where(z > 0, t * r, r)
        log_sig = jnp.minimum(z, 0.0) - jnp.log1p(t)
        c = log1m_lb[d] + log_sig
        a_ = log_lb[d]
        log_f = jnp.maximum(a_, c) + jnp.log1p(jnp.exp(-jnp.abs(a_ - c)))
        return one_m_lb[d] * sig_neg, log_f

    def one_direction(base, f_ref, d, st_ref, o_ref):
        q, kk, v, lg = [], [], [], []
        for j in range(n):
            zq = load(q_ref, base, j)
            q.append(zq * _sigmoid(zq) * (HG_DK ** -0.5))
            v.append(load(i_ref, base, j))
            k_j, g_j = gates(load(f_ref, base, j), d)
            kk.append(k_j)
            lg.append(g_j)
        o = _hgrn_chunk(q, kk, v, lg, st_ref, masks_ref, wsum_ref, d == 1, row)
        for j in range(n):
            o_ref[pl.ds(base + j, n, stride=n), :] = o[j]

    def body(c, carry):
        one_direction(pl.multiple_of(c * HG_CHUNK, HG_CHUNK), ff_ref, 0, sfw_ref, ofw_ref)
        one_direction(pl.multiple_of((nchunks - 1 - c) * HG_CHUNK, HG_CHUNK),
                      fb_ref, 1, sbw_ref, obw_ref)
        return carry

    lax.fori_loop(0, nchunks, body, 0)

    gn = gn_ref[...]
    rows_e = 256

    def epilogue(c, carry):
        sl = pl.ds(pl.multiple_of(c * rows_e, rows_e), rows_e)
        o = ofw_ref[sl, :] + obw_ref[sl, :]
        g = g_ref[sl, :]
        out_ref[sl, :] = (_rms(o, gn) * (g * _sigmoid(g))).astype(out_ref.dtype)
        return carry

    lax.fori_loop(0, seq // rows_e, epilogue, 0)


def _hgrn(hg3d, lb_logits2d, gnorm, layer):
    B, L, _ = hg3d.shape
    masks, wsum = _hgrn_consts()

    def col(c):
        return pl.BlockSpec((None, L, HG_DK), lambda b, h, c=c: (b, 0, c * HG_HEADS + h))

    return pl.pallas_call(
        functools.partial(_hgrn_kernel, layer=layer, seq=L),
        grid=(B, HG_HEADS),
        in_specs=[
            col(0), col(1), col(2), col(3), col(4),
            pl.BlockSpec((2 * DEPTH, HG_DK), lambda b, h: (0, h)),
            _const_spec((None, 1, HG_DK), lambda b, h: (layer, 0, 0)),
            _const_spec(masks.shape, lambda b, h: (0, 0, 0)),
            _const_spec(wsum.shape, lambda b, h: (0, 0)),
        ],
        out_specs=pl.BlockSpec((None, L, HG_DK), lambda b, h: (b, 0, h)),
        out_shape=jax.ShapeDtypeStruct((B, L, HG_WIDTH), BF16),
        scratch_shapes=[
            pltpu.VMEM((L, HG_DK), F32),
            pltpu.VMEM((L, HG_DK), F32),
            pltpu.VMEM((HG_DK, HG_DK), F32),
            pltpu.VMEM((HG_DK, HG_DK), F32),
        ],
        compiler_params=pltpu.CompilerParams(
            dimension_semantics=("parallel", "parallel"), vmem_limit_bytes=VMEM_LIMIT),
        name="hgrn2",
    )(hg3d, hg3d, hg3d, hg3d, hg3d, lb_logits2d, gnorm, masks, wsum)


def _attn_kernel(q_ref, k_ref, v_ref, lam_ref, sw_ref, out_ref, *, lambda_init):
    lp = lam_ref[...]
    lam = (jnp.exp(jnp.sum(lp[0:1] * lp[1:2], axis=-1, keepdims=True))
           - jnp.exp(jnp.sum(lp[2:3] * lp[3:4], axis=-1, keepdims=True)) + lambda_init)
    q = q_ref[...]
    k = k_ref[...]
    lane = lax.broadcasted_iota(jnp.int32, q.shape, 1)
    zero = jnp.zeros_like(q)
    s1 = _dot_nt(jnp.where(lane < DA_DH, q, zero), k)
    s2 = _dot_nt(jnp.where(lane >= DA_DH, q, zero), k)
    p1 = jnp.exp(s1 - jnp.max(s1, axis=-1, keepdims=True))
    p2 = jnp.exp(s2 - jnp.max(s2, axis=-1, keepdims=True))
    r1 = 1.0 / jnp.sum(p1, axis=-1, keepdims=True)
    r2 = lam / jnp.sum(p2, axis=-1, keepdims=True)
    a = p1 * r1 - p2 * r2
    o = _dot(a.astype(BF16), v_ref[...])
    out_ref[...] = (_rms(o, sw_ref[...]) * (1.0 - lambda_init)).astype(out_ref.dtype)


def _attention(q3d, k3d, v3d, lam_params, subln, layer, tq):
    B, L, _ = q3d.shape
    lambda_init = 0.8 - 0.6 * math.exp(-0.3 * layer)
    return pl.pallas_call(
        functools.partial(_attn_kernel, lambda_init=lambda_init),
        grid=(B, DA_HEADS, L // tq),
        in_specs=[
            pl.BlockSpec((None, tq, DA_DV), lambda b, h, i: (b, i, h)),
            pl.BlockSpec((None, L, DA_DV), lambda b, h, i: (b, 0, h)),
            pl.BlockSpec((None, L, DA_DV), lambda b, h, i: (b, 0, h)),
            _const_spec((None, 4, DA_DH), lambda b, h, i: (layer, 0, 0)),
            _const_spec((None, 1, DA_DV), lambda b, h, i: (layer, 0, 0)),
        ],
        out_specs=pl.BlockSpec((None, tq, DA_DV), lambda b, h, i: (b, i, h)),
        out_shape=jax.ShapeDtypeStruct((B, L, DA_WIDTH), BF16),
        compiler_params=pltpu.CompilerParams(
            dimension_semantics=("parallel", "parallel", "parallel"),
            vmem_limit_bytes=VMEM_LIMIT),
        name="diff_attn",
    )(q3d, k3d, v3d, lam_params, subln)


def _mix_kernel(x_ref, a_ref, b_ref, gate_ref, wa_ref, wb_ref, wo_ref, nw_ref, out_ref):
    ga = gate_ref[:, :D_MODEL].astype(F32)
    gb = gate_ref[:, D_MODEL:].astype(F32)
    m = ga * _dot(a_ref[...], wa_ref[...]) + gb * _dot(b_ref[...], wb_ref[...])
    r = _dot(m.astype(BF16), wo_ref[...])
    out_ref[...] = x_ref[...] + _rms(r, nw_ref[...])


def _mix(x2d, a2d, b2d, gate2d, wa, wb, wo, norm_w, layer, tm):
    T = x2d.shape[0]
    return pl.pallas_call(
        _mix_kernel,
        grid=(T // tm,),
        in_specs=[
            pl.BlockSpec((tm, D_MODEL), lambda i: (i, 0)),
            pl.BlockSpec((tm, HG_WIDTH), lambda i: (i, 0)),
            pl.BlockSpec((tm, DA_WIDTH), lambda i: (i, 0)),
            pl.BlockSpec((tm, 2 * D_MODEL), lambda i: (i, 0)),
            _const_spec((None, HG_WIDTH, D_MODEL), lambda i: (layer, 0, 0)),
            _const_spec((None, DA_WIDTH, D_MODEL), lambda i: (layer, 0, 0)),
            _const_spec((None, D_MODEL, D_MODEL), lambda i: (layer, 0, 0)),
            _const_spec((None, 1, D_MODEL), lambda i: (layer, 0, 0)),
        ],
        out_specs=pl.BlockSpec((tm, D_MODEL), lambda i: (i, 0)),
        out_shape=jax.ShapeDtypeStruct((T, D_MODEL), F32),
        compiler_params=pltpu.CompilerParams(
            dimension_semantics=("parallel",), vmem_limit_bytes=VMEM_LIMIT),
        name="mix_out",
    )(x2d, a2d, b2d, gate2d, wa, wb, wo, norm_w)


FFN_HALO = 16
FFN_TF = 256


def _gelu_tanh(x):
    return 0.5 * x * (1.0 + jnp.tanh(math.sqrt(2.0 / math.pi) * (x + 0.044715 * (x * x * x))))


def _ffn_kernel(x_ref, xp_ref, xn_ref, p_ref, nwpre_ref, wup_ref, cw_ref, cb_ref, wdn_ref,
                nwpost_ref, wple_ref, wpg_ref, nwple_ref, out_ref, h_ref, acc_ref,
                *, tm, tiles_per_seq):
    i = pl.program_id(0)
    pos = i % tiles_per_seq
    nw = nwpre_ref[...]
    x = x_ref[...]
    hp = _rms(xp_ref[...], nw) * jnp.where(pos == 0, 0.0, 1.0)
    hn = _rms(xn_ref[...], nw) * jnp.where(pos == tiles_per_seq - 1, 0.0, 1.0)
    h_ref[0:FFN_HALO, :] = hp.astype(BF16)
    h_ref[FFN_HALO:FFN_HALO + tm, :] = _rms(x, nw).astype(BF16)
    h_ref[FFN_HALO + tm:, :] = hn.astype(BF16)
    rows = tm + 2 * FFN_HALO
    mid = slice(FFN_HALO, FFN_HALO + tm)

    def conv(u, col):
        cw = cw_ref[:, col]
        up = pltpu.roll(u, 1, axis=0)[mid]
        dn = pltpu.roll(u, rows - 1, axis=0)[mid]
        return up * cw[0:1] + u[mid] * cw[1:2] + dn * cw[2:3] + cb_ref[:, col]

    for c in range(D_FF // FFN_TF):
        gcol = slice(c * FFN_TF, (c + 1) * FFN_TF)
        vcol = slice(D_FF + c * FFN_TF, D_FF + (c + 1) * FFN_TF)
        h = h_ref[...]
        gate = conv(_dot(h, wup_ref[:, gcol]), gcol)
        val = conv(_dot(h, wup_ref[:, vcol]), vcol)
        act = (_gelu_tanh(gate) * val).astype(BF16)
        contrib = _dot(act, wdn_ref[gcol, :])
        if c == 0:
            acc_ref[...] = contrib
        else:
            acc_ref[...] += contrib

    x2 = x + _rms(acc_ref[...], nwpost_ref[...])
    e = _dot(p_ref[...].astype(BF16), wple_ref[...])
    g = _sigmoid(_dot(_rms(x2).astype(BF16), wpg_ref[...]))
    out_ref[...] = x2 + _rms(g * e, nwple_ref[...])


def _ffn_ple(x2d, p2d, nw_pre, w_up, conv_w, conv_b, w_down, nw_post,
             w_ple, w_ple_gate, nw_ple, layer, seq, tm):
    T = x2d.shape[0]
    tiles_per_seq = seq // tm
    hb = tm // FFN_HALO
    nhalo = T // FFN_HALO

    def lw(shape):
        return _const_spec((None,) + shape, lambda i: (layer,) + (0,) * len(shape))

    return pl.pallas_call(
        functools.partial(_ffn_kernel, tm=tm, tiles_per_seq=tiles_per_seq),
        grid=(T // tm,),
        in_specs=[
            pl.BlockSpec((tm, D_MODEL), lambda i: (i, 0)),
            pl.BlockSpec((FFN_HALO, D_MODEL), lambda i: (jnp.maximum(i * hb - 1, 0), 0)),
            pl.BlockSpec((FFN_HALO, D_MODEL), lambda i: (jnp.minimum((i + 1) * hb, nhalo - 1), 0)),
            pl.BlockSpec((None, tm, PLE_DIM), lambda i: (layer, i, 0)),
            lw((1, D_MODEL)),
            lw((D_MODEL, 2 * D_FF)),
            lw((3, 2 * D_FF)),
            lw((1, 2 * D_FF)),
            lw((D_FF, D_MODEL)),
            lw((1, D_MODEL)),
            lw((PLE_DIM, D_MODEL)),
            lw((D_MODEL, D_MODEL)),
            lw((1, D_MODEL)),
        ],
        out_specs=pl.BlockSpec((tm, D_MODEL), lambda i: (i, 0)),
        out_shape=jax.ShapeDtypeStruct((T, D_MODEL), F32),
        scratch_shapes=[
            pltpu.VMEM((tm + 2 * FFN_HALO, D_MODEL), BF16),
            pltpu.VMEM((tm, D_MODEL), F32),
        ],
        compiler_params=pltpu.CompilerParams(
            dimension_semantics=("parallel",), vmem_limit_bytes=VMEM_LIMIT),
        name="convffn_ple",
    )(x2d, x2d, x2d, p2d, nw_pre, w_up, conv_w, conv_b, w_down, nw_post,
      w_ple, w_ple_gate, nw_ple)


def _rope_tables(L):
    inv = 1.0 / (ROPE_THETA ** (jnp.arange(0, DA_DH, 2, dtype=F32) / DA_DH))
    ang = jnp.arange(L, dtype=F32)[:, None] * inv[None, :]
    cos = jnp.tile(jnp.cos(ang), (1, 4))
    sin = jnp.sin(ang)
    sin_signed = jnp.tile(jnp.concatenate([-sin, sin], axis=-1), (1, 2))
    return cos, sin_signed


def _trunk(x, p, w, tm_in, tm_mix, tm_ffn, tq):
    B, L, _ = x.shape
    T = B * L
    cos, sin = _rope_tables(L)
    x2d = x.reshape(T, D_MODEL)
    p3d = p.reshape(DEPTH, T, PLE_DIM)
    for l in range(DEPTH):
        hg, q, k, v, gate = _in_proj(x2d, w["norm_mix_pre"], w["w_in"], cos, sin, l, L, tm_in)
        a = _hgrn(hg.reshape(B, L, 5 * HG_WIDTH), w["lb_logits"], w["hgrn_gnorm"], l)
        bb = _attention(q.reshape(B, L, DA_WIDTH), k.reshape(B, L, DA_WIDTH),
                        v.reshape(B, L, DA_WIDTH), w["diff_lambda"], w["diff_subln"], l, tq)
        x2d = _mix(x2d, a.reshape(T, HG_WIDTH), bb.reshape(T, DA_WIDTH), gate,
                   w["w_branch_a"], w["w_branch_b"], w["w_out"], w["norm_mix_post"], l, tm_mix)
        x2d = _ffn_ple(x2d, p3d, w["norm_ffn_pre"], w["w_up"], w["conv_w"], w["conv_b"],
                       w["w_down"], w["norm_ffn_post"], w["w_ple"], w["w_ple_gate"],
                       w["norm_ple"], l, L, tm_ffn)
    return x2d.reshape(B, L, D_MODEL)


def _prep_weights(w_in, hgrn_lb_logits, hgrn_gnorm, diff_lambda, diff_subln, w_branch_a,
                  w_branch_b, w_out, norm_mix_pre, norm_mix_post, w_up, conv_w, conv_b, w_down,
                  norm_ffn_pre, norm_ffn_post, w_ple, w_ple_gate, norm_ple):
    def vec(a):
        return a.reshape(DEPTH, 1, a.shape[-1]).astype(F32)

    return {
        "w_in": w_in.astype(BF16),
        "lb_logits": hgrn_lb_logits.reshape(2 * DEPTH, HG_WIDTH).astype(F32),
        "hgrn_gnorm": vec(hgrn_gnorm),
        "diff_lambda": diff_lambda.astype(F32),
        "diff_subln": vec(diff_subln),
        "w_branch_a": w_branch_a.astype(BF16),
        "w_branch_b": w_branch_b.astype(BF16),
        "w_out": w_out.astype(BF16),
        "norm_mix_pre": vec(norm_mix_pre),
        "norm_mix_post": vec(norm_mix_post),
        "w_up": w_up.astype(BF16),
        "conv_w": conv_w.astype(F32),
        "conv_b": vec(conv_b),
        "w_down": w_down.astype(BF16),
        "norm_ffn_pre": vec(norm_ffn_pre),
        "norm_ffn_post": vec(norm_ffn_post),
        "w_ple": w_ple.astype(BF16),
        "w_ple_gate": w_ple_gate.astype(BF16),
        "norm_ple": vec(norm_ple),
    }


def _tile(n, pref):
    t = min(n, pref)
    assert n % t == 0
    return t


def kernel(x_prompt, x_sample, p_prompt, p_sample, w_in, hgrn_lb_logits, hgrn_gnorm, diff_lambda, diff_subln, w_branch_a, w_branch_b, w_out, norm_mix_pre, norm_mix_post, w_up, conv_w, conv_b, w_down, norm_ffn_pre, norm_ffn_post, w_ple, w_ple_gate, norm_ple):
    w = _prep_weights(w_in, hgrn_lb_logits, hgrn_gnorm, diff_lambda, diff_subln, w_branch_a,
                      w_branch_b, w_out, norm_mix_pre, norm_mix_post, w_up, conv_w, conv_b,
                      w_down, norm_ffn_pre, norm_ffn_post, w_ple, w_ple_gate, norm_ple)
    outs = []
    for x, p in ((x_prompt, p_prompt), (x_sample, p_sample)):
        L = x.shape[1]
        outs.append(_trunk(x, p, w, tm_in=_tile(L, 512), tm_mix=_tile(L, 512),
                           tm_ffn=_tile(L, 512), tq=_tile(L, 256)))
    return tuple(outs)
```

```python
import functools
import math

import jax
import jax.numpy as jnp
from jax import lax
from jax.experimental import pallas as pl
from jax.experimental.pallas import tpu as pltpu

D_MODEL = 1024
DEPTH = 4
PLE_DIM = 256
HG_WIDTH = D_MODEL // 2
HG_HEADS = 4
HG_DK = HG_WIDTH // HG_HEADS
LB_FLOOR = 1e-30
DA_WIDTH = D_MODEL // 2
DA_HEADS = 4
DA_DV = DA_WIDTH // DA_HEADS
DA_DH = DA_DV // 2
D_FF = 2816
ROPE_THETA = 10000.0
EPS = 1e-6
IN_COLS = 5 * HG_WIDTH + 3 * DA_WIDTH + 2 * D_MODEL

LANES = 128
SUBLANES = 8
HG_CHUNK = SUBLANES * SUBLANES
VMEM_LIMIT = 56 * 1024 * 1024

BF16 = jnp.bfloat16
F32 = jnp.float32
LOG2E = math.log2(math.e)


def _dot(a, b):
    return jnp.dot(a, b, preferred_element_type=F32)


def _dot_nt(a, b):
    return lax.dot_general(a, b, (((1,), (1,)), ((), ())), preferred_element_type=F32)


def _dot_tn(a, b):
    return lax.dot_general(a, b, (((0,), (0,)), ((), ())), preferred_element_type=F32)


def _rms(x, w=None):
    y = x * lax.rsqrt(jnp.mean(x * x, axis=-1, keepdims=True) + EPS)
    return y if w is None else y * w


def _sigmoid(x):
    return 1.0 / (1.0 + jnp.exp(-x))


def _const_spec(block_shape, index_map):
    return pl.BlockSpec(block_shape, index_map, pipeline_mode=pl.Buffered(1))


def _to_slabs(slab_ref, val):
    for s in range(val.shape[1] // LANES):
        slab_ref[s] = val[:, s * LANES:(s + 1) * LANES]


def _chunk_transposed(slab_ref, rows):
    n = SUBLANES
    return jnp.concatenate(
        [jnp.concatenate([slab_ref[s, pl.ds(c * HG_CHUNK + j, n, stride=n), :]
                          for s in range(slab_ref.shape[0])], axis=1)
         for c in range(rows // HG_CHUNK) for j in range(n)], axis=0)


def _rope(z, cos, sin_signed, first_half):
    rot = jnp.where(first_half, pltpu.roll(z, 96, axis=1), pltpu.roll(z, 32, axis=1))
    return z * cos + rot * sin_signed


def _lower_bounds(lbl_ref, layer):
    out = []
    for d in range(2):
        rows = [lbl_ref[2 * l + d:2 * l + d + 1, :] for l in range(DEPTH)]
        mx = functools.reduce(jnp.maximum, rows)
        ex = [jnp.exp(r - mx) for r in rows]
        den = functools.reduce(lambda a_, b_: a_ + b_, ex)
        sm = [e / den for e in ex]
        cs = sm[0]
        for l in range(1, layer + 1):
            cs = cs + sm[l]
        out.append(cs - sm[0])
    return out


def _hgrn_gates(z, lb):
    t = jnp.exp(-jnp.abs(z))
    r = 1.0 / (1.0 + t)
    sig_neg = jnp.where(z > 0, t * r, r)
    log_sig = jnp.minimum(z, 0.0) - jnp.log1p(t)
    c = jnp.log1p(-lb) + log_sig
    a = jnp.log(jnp.maximum(lb, LB_FLOOR))
    log_f = jnp.maximum(a, c) + jnp.log1p(jnp.exp(-jnp.abs(a - c)))
    return (1.0 - lb) * sig_neg, log_f * LOG2E


def _in_proj_kernel(x_ref, nw_ref, w_ref, cos_ref, sin_ref, lbl_ref,
                    qh_ref, vh_ref, kf_ref, kb_ref, lgf_ref, lgb_ref, gg_ref,
                    q_ref, k_ref, v_ref, gate_ref, hn_ref, *, layer, tm):
    hn = _rms(x_ref[...], nw_ref[...])
    h = hn.astype(BF16)
    _to_slabs(hn_ref, hn)
    hp = _chunk_transposed(hn_ref, tm).astype(BF16)
    cw = HG_WIDTH
    lb = _lower_bounds(lbl_ref, layer)
    zq = _dot(hp, w_ref[:, 0:cw])
    qh_ref[...] = (zq * _sigmoid(zq) * (HG_DK ** -0.5)).astype(BF16)
    kf, lgf = _hgrn_gates(_dot(hp, w_ref[:, cw:2 * cw]), lb[0])
    kf_ref[...] = kf.astype(BF16)
    lgf_ref[...] = lgf
    kb, lgb = _hgrn_gates(_dot(hp, w_ref[:, 2 * cw:3 * cw]), lb[1])
    kb_ref[...] = kb.astype(BF16)
    lgb_ref[...] = lgb
    vh_ref[...] = _dot(hp, w_ref[:, 3 * cw:4 * cw]).astype(BF16)
    zg = _dot(h, w_ref[:, 4 * cw:5 * cw])
    gg_ref[...] = (zg * _sigmoid(zg)).astype(BF16)
    cos = cos_ref[...]
    sin = sin_ref[...]
    lane = lax.broadcasted_iota(jnp.int32, cos.shape, 1)
    first_half = (lane & (DA_DH - 1)) < (DA_DH // 2)
    base = 5 * HG_WIDTH
    zq = _dot(h, w_ref[:, base:base + DA_WIDTH])
    zk = _dot(h, w_ref[:, base + DA_WIDTH:base + 2 * DA_WIDTH])
    for hd in range(DA_HEADS):
        sl = slice(hd * LANES, (hd + 1) * LANES)
        q_ref[:, sl] = (_rope(zq[:, sl], cos, sin, first_half) * (LOG2E * DA_DH ** -0.5)).astype(BF16)
        k_ref[:, sl] = _rope(zk[:, sl], cos, sin, first_half).astype(BF16)
    v_ref[...] = _dot(h, w_ref[:, base + 2 * DA_WIDTH:base + 3 * DA_WIDTH]).astype(BF16)
    gbase = base + 3 * DA_WIDTH
    for c in range(4):
        zg = _dot(h, w_ref[:, gbase + c * cw:gbase + (c + 1) * cw])
        gate_ref[:, c * cw:(c + 1) * cw] = _sigmoid(zg).astype(BF16)


def _in_proj(x2d, norm_w, w_in, cos, sin, lb_logits2d, layer, seq, tm):
    T = x2d.shape[0]
    npos = seq // tm
    half = lambda dt: (pl.BlockSpec((tm, HG_WIDTH), lambda i: (i, 0)),
                       jax.ShapeDtypeStruct((T, HG_WIDTH), dt))
    outs = [half(BF16), half(BF16), half(BF16), half(BF16), half(F32), half(F32), half(BF16),
            half(BF16), half(BF16), half(BF16),
            (pl.BlockSpec((tm, 2 * D_MODEL), lambda i: (i, 0)),
             jax.ShapeDtypeStruct((T, 2 * D_MODEL), BF16))]
    return pl.pallas_call(
        functools.partial(_in_proj_kernel, layer=layer, tm=tm),
        grid=(T // tm,),
        in_specs=[
            pl.BlockSpec((tm, D_MODEL), lambda i: (i, 0)),
            _const_spec((None, 1, D_MODEL), lambda i: (layer, 0, 0)),
            _const_spec((None, D_MODEL, IN_COLS), lambda i: (layer, 0, 0)),
            pl.BlockSpec((tm, LANES), lambda i: (i % npos, 0)),
            pl.BlockSpec((tm, LANES), lambda i: (i % npos, 0)),
            _const_spec((2 * DEPTH, HG_WIDTH), lambda i: (0, 0)),
        ],
        out_specs=[o[0] for o in outs],
        out_shape=[o[1] for o in outs],
        scratch_shapes=[pltpu.VMEM((D_MODEL // LANES, tm, LANES), F32)],
        compiler_params=pltpu.CompilerParams(
            dimension_semantics=("parallel",), vmem_limit_bytes=VMEM_LIMIT),
        name="in_proj",
    )(x2d, norm_w, w_in, cos, sin, lb_logits2d)


def _hgrn_consts():
    blk = jnp.arange(SUBLANES)[:, None]
    cblk = (jnp.arange(HG_CHUNK) % SUBLANES)[None, :]
    masks = []
    for rev in (False, True):
        for m in (1, 2, 4):
            same = (blk // (2 * m)) == (cblk // (2 * m))
            r_hi = (blk % (2 * m)) >= m
            c_hi = (cblk % (2 * m)) >= m
            mk = same & (r_hi & ~c_hi if not rev else ~r_hi & c_hi)
            masks.append(mk)
    masks.append(blk == cblk)
    masks = jnp.stack(masks).astype(F32)
    grp = (jnp.arange(SUBLANES * LANES) // LANES)[:, None]
    cgrp = (jnp.arange(HG_CHUNK) // SUBLANES)[None, :]
    wsum = (grp == cgrp).astype(BF16)
    return masks, wsum


def _gather_rows(x, src, row):
    out = None
    for s in sorted({(i - src[i]) % SUBLANES for i in range(SUBLANES)}):
        rolled = x if s == 0 else pltpu.roll(x, s, axis=0)
        if out is None:
            out = rolled
        else:
            sel = functools.reduce(
                jnp.logical_or,
                [row == i for i in range(SUBLANES) if (i - src[i]) % SUBLANES == s])
            out = jnp.where(sel, rolled, out)
    return out


def _hgrn_prep(qb, kb, vb, lgc, rev, row):
    n = SUBLANES
    split = lambda a_: [a_[j * n:(j + 1) * n, :] for j in range(n)]
    q = split(qb.astype(F32))
    kk = split(kb.astype(F32))
    lg = split(lgc)
    order = list(range(n - 1, -1, -1)) if rev else list(range(n))
    w = [None] * n
    acc = None
    for j in order:
        acc = lg[j] if acc is None else acc + lg[j]
        w[j] = acc
    tot = acc
    incl = tot
    for s in (1, 2, 4):
        if not rev:
            incl = incl + jnp.where(row >= s, pltpu.roll(incl, s, axis=0), 0.0)
        else:
            incl = incl + jnp.where(row < n - s, pltpu.roll(incl, n - s, axis=0), 0.0)
    if not rev:
        excl = jnp.where(row >= 1, pltpu.roll(incl, 1, axis=0), 0.0)
        chunk_tot = jnp.broadcast_to(incl[n - 1:n, :], incl.shape)
    else:
        excl = jnp.where(row < n - 1, pltpu.roll(incl, n - 1, axis=0), 0.0)
        chunk_tot = jnp.broadcast_to(incl[0:1, :], incl.shape)
    b = [w[j] + excl for j in range(n)]

    bcat = jnp.concatenate(b, axis=0)

    c = {"vb": vb, "rev": rev}
    c["qe"] = qb * jnp.exp2(bcat).astype(BF16)

    c["lv"] = []
    for m in (1, 2, 4):
        if not rev:
            src = [(i // (2 * m)) * 2 * m + m - 1 for i in range(n)]
        else:
            src = [(i // (2 * m)) * 2 * m + m for i in range(n)]
        y = excl - _gather_rows(incl, src, row)
        e = jnp.concatenate([jnp.exp2(-jnp.abs(w[j] + y)) for j in range(n)], axis=0).astype(BF16)
        c["lv"].append((qb * e, kb * e))

    zero = jnp.zeros_like(q[0])
    rows_p = []
    for j in range(n):
        blocks = []
        for jp in range(n):
            earlier = (jp <= j) if not rev else (jp >= j)
            if not earlier:
                blocks.append(zero)
            elif jp == j:
                blocks.append(q[j] * kk[j])
            else:
                blocks.append(q[j] * kk[jp] * jnp.exp2(w[j] - w[jp]))
        rows_p.append(jnp.concatenate(blocks, axis=1))
    c["pbig"] = jnp.concatenate(rows_p, axis=0).astype(BF16)

    tot_cat = jnp.concatenate([chunk_tot] * n, axis=0)
    c["kd"] = kb * jnp.exp2(tot_cat - bcat).astype(BF16)
    c["dec"] = jnp.exp2(chunk_tot[0:1, :])
    return c


def _hgrn_mm1(c, st_ref, wsum_ref):
    st = st_ref[...]
    c["st"] = st
    c["o"] = _dot_nt(c["qe"], st.astype(BF16))
    c["am"] = [_dot_nt(qm, km) for qm, km in c["lv"]]
    c["ad"] = _dot(c["pbig"], wsum_ref[...])
    c["kv"] = _dot_tn(c["vb"], c["kd"])


def _hgrn_mid(c, masks_ref):
    n = SUBLANES
    a = c["ad"].reshape(n, n, HG_CHUNK) * masks_ref[6][None]
    for li, am in enumerate(c["am"]):
        mk = masks_ref[(3 if c["rev"] else 0) + li]
        a = a + am.reshape(n, n, HG_CHUNK) * mk[None]
    c["a"] = a.reshape(HG_CHUNK, HG_CHUNK).astype(BF16)


def _hgrn_mm2(c, st_ref):
    st_ref[...] = c["dec"] * c["st"] + c["kv"]
    return c["o"] + _dot(c["a"], c["vb"])


def _hgrn_kernel(qh_ref, vh_ref, kf_ref, kb_ref, lgf_ref, lgb_ref, gg_ref, gn_ref, masks_ref,
                 wsum_ref, out_ref, o_ref, st_ref, *, seq, heads):
    n = SUBLANES
    nchunks = seq // HG_CHUNK
    row = lax.broadcasted_iota(jnp.int32, (n, HG_DK), 0)
    st_ref[...] = jnp.zeros_like(st_ref)

    def make_body(accumulate):
        def body(c, carry):
            fw = pl.multiple_of(c * HG_CHUNK, HG_CHUNK)
            bw = pl.multiple_of((nchunks - 1 - c) * HG_CHUNK, HG_CHUNK)
            chains = [(hd, d) for hd in range(heads) for d in range(2)]
            ctx = []
            for hd, d in chains:
                rows = pl.ds(bw if d else fw, HG_CHUNK)
                cols = slice(hd * HG_DK, (hd + 1) * HG_DK)
                k_ref, lg_ref = (kb_ref, lgb_ref) if d else (kf_ref, lgf_ref)
                ctx.append(_hgrn_prep(qh_ref[rows, cols], k_ref[rows, cols], vh_ref[rows, cols],
                                      lg_ref[rows, cols], d == 1, row))
            for (hd, d), cx in zip(chains, ctx):
                _hgrn_mm1(cx, st_ref.at[2 * hd + d], wsum_ref)
            for cx in ctx:
                _hgrn_mid(cx, masks_ref)
            for (hd, d), cx in zip(chains, ctx):
                o = _hgrn_mm2(cx, st_ref.at[2 * hd + d])
                base = bw if d else fw
                for j in range(n):
                    dst = (hd, pl.ds(base + j, n, stride=n), slice(None))
                    piece = o[j * n:(j + 1) * n, :]
                    o_ref[dst] = o_ref[dst] + piece if accumulate else piece
            return carry
        return body

    lax.fori_loop(0, nchunks // 2, make_body(False), 0)
    lax.fori_loop(nchunks // 2, nchunks, make_body(True), 0)

    gn = gn_ref[...]
    rows_e = min(256, seq)

    def epilogue(c, carry):
        sl = pl.ds(pl.multiple_of(c * rows_e, rows_e), rows_e)
        for hd in range(heads):
            cols = slice(hd * HG_DK, (hd + 1) * HG_DK)
            y = _rms(o_ref[hd, sl, :], gn) * gg_ref[sl, cols].astype(F32)
            out_ref[sl, cols] = y.astype(out_ref.dtype)
        return carry

    lax.fori_loop(0, seq // rows_e, epilogue, 0)


def _hgrn(qh, vh, kf, kb, lgf, lgb, gg, gnorm, layer, heads):
    B, L, _ = qh.shape
    masks, wsum = _hgrn_consts()
    width = heads * HG_DK
    blk = pl.BlockSpec((None, L, width), lambda b, h: (b, 0, h))
    return pl.pallas_call(
        functools.partial(_hgrn_kernel, seq=L, heads=heads),
        grid=(B, HG_HEADS // heads),
        in_specs=[
            blk, blk, blk, blk, blk, blk, blk,
            _const_spec((None, 1, HG_DK), lambda b, h: (layer, 0, 0)),
            _const_spec(masks.shape, lambda b, h: (0, 0, 0)),
            _const_spec(wsum.shape, lambda b, h: (0, 0)),
        ],
        out_specs=blk,
        out_shape=jax.ShapeDtypeStruct((B, L, HG_WIDTH), BF16),
        scratch_shapes=[
            pltpu.VMEM((heads, L, HG_DK), F32),
            pltpu.VMEM((2 * heads, HG_DK, HG_DK), F32),
        ],
        compiler_params=pltpu.CompilerParams(
            dimension_semantics=("parallel", "parallel"), vmem_limit_bytes=VMEM_LIMIT),
        name="hgrn2",
    )(qh, vh, kf, kb, lgf, lgb, gg, gnorm, masks, wsum)


ATTN_SUB = 256


def _attn_kernel(q_ref, k_ref, v_ref, lam_ref, sw_ref, out_ref, *, lambda_init):
    lp = lam_ref[...]
    lam = (jnp.exp(jnp.sum(lp[0:1] * lp[1:2], axis=-1, keepdims=True))
           - jnp.exp(jnp.sum(lp[2:3] * lp[3:4], axis=-1, keepdims=True)) + lambda_init)
    k = k_ref[...]
    v = v_ref[...]
    sw = sw_ref[...]
    tq = q_ref.shape[0]
    sub = min(ATTN_SUB, tq)
    for r in range(tq // sub):
        rows = slice(r * sub, (r + 1) * sub)
        q = q_ref[rows, :]
        lane = lax.broadcasted_iota(jnp.int32, q.shape, 1)
        zero = jnp.zeros_like(q)
        p, l = [], []
        for qi in (jnp.where(lane < DA_DH, q, zero), jnp.where(lane >= DA_DH, q, zero)):
            s = _dot_nt(qi, k)
            e = jnp.exp2(s - jnp.max(s, axis=-1, keepdims=True))
            p.append(e)
            l.append(jnp.sum(e, axis=-1, keepdims=True))
        a = p[0] * (1.0 / l[0]) - p[1] * (lam / l[1])
        o = _dot(a.astype(BF16), v)
        out_ref[rows, :] = (_rms(o, sw) * (1.0 - lambda_init)).astype(out_ref.dtype)


def _attention(q3d, k3d, v3d, lam_params, subln, layer, tq):
    B, L, _ = q3d.shape
    lambda_init = 0.8 - 0.6 * math.exp(-0.3 * layer)
    return pl.pallas_call(
        functools.partial(_attn_kernel, lambda_init=lambda_init),
        grid=(B, DA_HEADS, L // tq),
        in_specs=[
            pl.BlockSpec((None, tq, DA_DV), lambda b, h, i: (b, i, h)),
            pl.BlockSpec((None, L, DA_DV), lambda b, h, i: (b, 0, h)),
            pl.BlockSpec((None, L, DA_DV), lambda b, h, i: (b, 0, h)),
            _const_spec((None, 4, DA_DH), lambda b, h, i: (layer, 0, 0)),
            _const_spec((None, 1, DA_DV), lambda b, h, i: (layer, 0, 0)),
        ],
        out_specs=pl.BlockSpec((None, tq, DA_DV), lambda b, h, i: (b, i, h)),
        out_shape=jax.ShapeDtypeStruct((B, L, DA_WIDTH), BF16),
        compiler_params=pltpu.CompilerParams(
            dimension_semantics=("parallel", "parallel", "parallel"),
            vmem_limit_bytes=VMEM_LIMIT),
        name="diff_attn",
    )(q3d, k3d, v3d, lam_params, subln)


def _mix_kernel(x_ref, a_ref, b_ref, gate_ref, wa_ref, wb_ref, wo_ref, nw_ref, out_ref):
    ga = gate_ref[:, :D_MODEL].astype(F32)
    gb = gate_ref[:, D_MODEL:].astype(F32)
    m = ga * _dot(a_ref[...], wa_ref[...]) + gb * _dot(b_ref[...], wb_ref[...])
    r = _dot(m.astype(BF16), wo_ref[...])
    out_ref[...] = x_ref[...] + _rms(r, nw_ref[...])


def _mix(x2d, a2d, b2d, gate2d, wa, wb, wo, norm_w, layer, tm):
    T = x2d.shape[0]
    return pl.pallas_call(
        _mix_kernel,
        grid=(T // tm,),
        in_specs=[
            pl.BlockSpec((tm, D_MODEL), lambda i: (i, 0)),
            pl.BlockSpec((tm, HG_WIDTH), lambda i: (i, 0)),
            pl.BlockSpec((tm, DA_WIDTH), lambda i: (i, 0)),
            pl.BlockSpec((tm, 2 * D_MODEL), lambda i: (i, 0)),
            _const_spec((None, HG_WIDTH, D_MODEL), lambda i: (layer, 0, 0)),
            _const_spec((None, DA_WIDTH, D_MODEL), lambda i: (layer, 0, 0)),
            _const_spec((None, D_MODEL, D_MODEL), lambda i: (layer, 0, 0)),
            _const_spec((None, 1, D_MODEL), lambda i: (layer, 0, 0)),
        ],
        out_specs=pl.BlockSpec((tm, D_MODEL), lambda i: (i, 0)),
        out_shape=jax.ShapeDtypeStruct((T, D_MODEL), F32),
        compiler_params=pltpu.CompilerParams(
            dimension_semantics=("parallel",), vmem_limit_bytes=VMEM_LIMIT),
        name="mix_out",
    )(x2d, a2d, b2d, gate2d, wa, wb, wo, norm_w)


FFN_HALO = 8
FFN_TF = 256


GELU_C = math.sqrt(2.0 / math.pi)
GELU_A = 0.044715


def _ffn_kernel(x_ref, xp_ref, xn_ref, p_ref, nwpre_ref, wup_ref, cw_ref, cb_ref, wdn_ref,
                nwpost_ref, wple_ref, wpg_ref, nwple_ref, out_ref, h_ref, slab_ref, act_ref,
                *, tm, tiles_per_seq):
    i = pl.program_id(0)
    pos = i % tiles_per_seq
    nw = nwpre_ref[...]
    x = x_ref[...]
    n = SUBLANES
    nch = tm // HG_CHUNK
    _to_slabs(slab_ref, _rms(x, nw))
    h_ref[0:tm, :] = _chunk_transposed(slab_ref, tm).astype(BF16)
    hp = _rms(xp_ref[...], nw) * jnp.where(pos == 0, 0.0, 1.0)
    hn = _rms(xn_ref[...], nw) * jnp.where(pos == tiles_per_seq - 1, 0.0, 1.0)
    h_ref[tm:, :] = jnp.concatenate([hp, hn], axis=0).astype(BF16)
    row = lax.broadcasted_iota(jnp.int32, (nch, n, FFN_TF), 1)

    def conv(u, col, scale):
        body = u[0:tm].reshape(nch, n, n, FFN_TF)
        before = u[tm:tm + n]
        after = u[tm + n:tm + 2 * n]
        first, last = body[:, 0], body[:, n - 1]
        prevs = jnp.concatenate([before[None], last[:-1]], axis=0)
        nexts = jnp.concatenate([first[1:], after[None]], axis=0)
        dn0 = jnp.where(row == 0, pltpu.roll(prevs, 1, axis=1), pltpu.roll(last, 1, axis=1))
        up7 = jnp.where(row == n - 1, pltpu.roll(nexts, n - 1, axis=1),
                        pltpu.roll(first, n - 1, axis=1))
        dn = jnp.concatenate([dn0[:, None], body[:, :n - 1]], axis=1)
        up = jnp.concatenate([body[:, 1:], up7[:, None]], axis=1)
        cw = cw_ref[:, col] * scale
        return dn * cw[0:1] + body * cw[1:2] + up * cw[2:3] + cb_ref[:, col] * scale

    for c in range(D_FF // FFN_TF):
        gcol = slice(c * FFN_TF, (c + 1) * FFN_TF)
        vcol = slice(D_FF + c * FFN_TF, D_FF + (c + 1) * FFN_TF)
        h = h_ref[...]
        g = conv(_dot(h, wup_ref[:, gcol]), gcol, 1.0)
        half_val = conv(_dot(h, wup_ref[:, vcol]), vcol, 0.5)
        t = jnp.tanh(g * (GELU_C + (GELU_C * GELU_A) * (g * g)))
        act_ref[:, gcol] = (g * (1.0 + t) * half_val).reshape(tm, FFN_TF).astype(BF16)

    _to_slabs(slab_ref, _dot(act_ref[...], wdn_ref[...]))
    x2 = x + _rms(_chunk_transposed(slab_ref, tm), nwpost_ref[...])
    e = _dot(p_ref[...].astype(BF16), wple_ref[...])
    g = _sigmoid(_dot(_rms(x2).astype(BF16), wpg_ref[...]))
    out_ref[...] = x2 + _rms(g * e, nwple_ref[...])


def _ffn_ple(x2d, p2d, nw_pre, w_up, conv_w, conv_b, w_down, nw_post,
             w_ple, w_ple_gate, nw_ple, layer, seq, tm):
    T = x2d.shape[0]
    tiles_per_seq = seq // tm
    hb = tm // FFN_HALO
    nhalo = T // FFN_HALO

    def lw(shape):
        return _const_spec((None,) + shape, lambda i: (layer,) + (0,) * len(shape))

    return pl.pallas_call(
        functools.partial(_ffn_kernel, tm=tm, tiles_per_seq=tiles_per_seq),
        grid=(T // tm,),
        in_specs=[
            pl.BlockSpec((tm, D_MODEL), lambda i: (i, 0)),
            pl.BlockSpec((FFN_HALO, D_MODEL), lambda i: (jnp.maximum(i * hb - 1, 0), 0)),
            pl.BlockSpec((FFN_HALO, D_MODEL), lambda i: (jnp.minimum((i + 1) * hb, nhalo - 1), 0)),
            pl.BlockSpec((None, tm, PLE_DIM), lambda i: (layer, i, 0)),
            lw((1, D_MODEL)),
            lw((D_MODEL, 2 * D_FF)),
            lw((3, 2 * D_FF)),
            lw((1, 2 * D_FF)),
            lw((D_FF, D_MODEL)),
            lw((1, D_MODEL)),
            lw((PLE_DIM, D_MODEL)),
            lw((D_MODEL, D_MODEL)),
            lw((1, D_MODEL)),
        ],
        out_specs=pl.BlockSpec((tm, D_MODEL), lambda i: (i, 0)),
        out_shape=jax.ShapeDtypeStruct((T, D_MODEL), F32),
        scratch_shapes=[
            pltpu.VMEM((tm + 2 * FFN_HALO, D_MODEL), BF16),
            pltpu.VMEM((D_MODEL // LANES, tm, LANES), F32),
            pltpu.VMEM((tm, D_FF), BF16),
        ],
        compiler_params=pltpu.CompilerParams(
            dimension_semantics=("parallel",), vmem_limit_bytes=VMEM_LIMIT),
        name="convffn_ple",
    )(x2d, x2d, x2d, p2d, nw_pre, w_up, conv_w, conv_b, w_down, nw_post,
      w_ple, w_ple_gate, nw_ple)


def _rope_tables(L):
    inv = 1.0 / (ROPE_THETA ** (jnp.arange(0, DA_DH, 2, dtype=F32) / DA_DH))
    ang = jnp.arange(L, dtype=F32)[:, None] * inv[None, :]
    cos = jnp.tile(jnp.cos(ang), (1, 4))
    sin = jnp.sin(ang)
    sin_signed = jnp.tile(jnp.concatenate([-sin, sin], axis=-1), (1, 2))
    return cos, sin_signed


def _trunk(x, p, w, tm_in, tm_mix, tm_ffn, tq, hg_heads=2):
    B, L, _ = x.shape
    T = B * L
    cos, sin = _rope_tables(L)
    x2d = x.reshape(T, D_MODEL)
    p3d = p.reshape(DEPTH, T, PLE_DIM)
    for l in range(DEPTH):
        (qh, vh, kf, kb, lgf, lgb, gg, q, k, v, gate) = _in_proj(
            x2d, w["norm_mix_pre"], w["w_in"], cos, sin, w["lb_logits"], l, L, tm_in)
        seq3 = lambda t: t.reshape(B, L, HG_WIDTH)
        a = _hgrn(seq3(qh), seq3(vh), seq3(kf), seq3(kb), seq3(lgf), seq3(lgb), seq3(gg),
                  w["hgrn_gnorm"], l, hg_heads)
        bb = _attention(q.reshape(B, L, DA_WIDTH), k.reshape(B, L, DA_WIDTH),
                        v.reshape(B, L, DA_WIDTH), w["diff_lambda"], w["diff_subln"], l, tq)
        x2d = _mix(x2d, a.reshape(T, HG_WIDTH), bb.reshape(T, DA_WIDTH), gate,
                   w["w_branch_a"], w["w_branch_b"], w["w_out"], w["norm_mix_post"], l, tm_mix)
        x2d = _ffn_ple(x2d, p3d, w["norm_ffn_pre"], w["w_up"], w["conv_w"], w["conv_b"],
                       w["w_down"], w["norm_ffn_post"], w["w_ple"], w["w_ple_gate"],
                       w["norm_ple"], l, L, tm_ffn)
    return x2d.reshape(B, L, D_MODEL)


def _prep_weights(w_in, hgrn_lb_logits, hgrn_gnorm, diff_lambda, diff_subln, w_branch_a,
                  w_branch_b, w_out, norm_mix_pre, norm_mix_post, w_up, conv_w, conv_b, w_down,
                  norm_ffn_pre, norm_ffn_post, w_ple, w_ple_gate, norm_ple):
    def vec(a):
        return a.reshape(DEPTH, 1, a.shape[-1]).astype(F32)

    return {
        "w_in": w_in.astype(BF16),
        "lb_logits": hgrn_lb_logits.reshape(2 * DEPTH, HG_WIDTH).astype(F32),
        "hgrn_gnorm": vec(hgrn_gnorm),
        "diff_lambda": diff_lambda.astype(F32),
        "diff_subln": vec(diff_subln),
        "w_branch_a": w_branch_a.astype(BF16),
        "w_branch_b": w_branch_b.astype(BF16),
        "w_out": w_out.astype(BF16),
        "norm_mix_pre": vec(norm_mix_pre),
        "norm_mix_post": vec(norm_mix_post),
        "w_up": w_up.astype(BF16),
        "conv_w": conv_w.astype(F32),
        "conv_b": vec(conv_b),
        "w_down": w_down.astype(BF16),
        "norm_ffn_pre": vec(norm_ffn_pre),
        "norm_ffn_post": vec(norm_ffn_post),
        "w_ple": w_ple.astype(BF16),
        "w_ple_gate": w_ple_gate.astype(BF16),
        "norm_ple": vec(norm_ple),
    }


def _tile(n, pref):
    t = min(n, pref)
    assert n % t == 0
    return t


def kernel(x_prompt, x_sample, p_prompt, p_sample, w_in, hgrn_lb_logits, hgrn_gnorm, diff_lambda, diff_subln, w_branch_a, w_branch_b, w_out, norm_mix_pre, norm_mix_post, w_up, conv_w, conv_b, w_down, norm_ffn_pre, norm_ffn_post, w_ple, w_ple_gate, norm_ple):
    w = _prep_weights(w_in, hgrn_lb_logits, hgrn_gnorm, diff_lambda, diff_subln, w_branch_a,
                      w_branch_b, w_out, norm_mix_pre, norm_mix_post, w_up, conv_w, conv_b,
                      w_down, norm_ffn_pre, norm_ffn_post, w_ple, w_ple_gate, norm_ple)
    outs = []
    for x, p in ((x_prompt, p_prompt), (x_sample, p_sample)):
        L = x.shape[1]
        outs.append(_trunk(x, p, w, tm_in=_tile(L, 512), tm_mix=_tile(L, 512),
                           tm_ffn=_tile(L, 512), tq=_tile(L, 512)))
    return tuple(outs)
```

```python
import functools
import math

import jax
import jax.numpy as jnp
from jax import lax
from jax.experimental import pallas as pl
from jax.experimental.pallas import tpu as pltpu

D_MODEL = 1024
DEPTH = 4
PLE_DIM = 256
HG_WIDTH = D_MODEL // 2
HG_HEADS = 4
HG_DK = HG_WIDTH // HG_HEADS
LB_FLOOR = 1e-30
DA_WIDTH = D_MODEL // 2
DA_HEADS = 4
DA_DV = DA_WIDTH // DA_HEADS
DA_DH = DA_DV // 2
D_FF = 2816
ROPE_THETA = 10000.0
EPS = 1e-6
IN_COLS = 5 * HG_WIDTH + 3 * DA_WIDTH + 2 * D_MODEL

LANES = 128
SUBLANES = 8
HG_CHUNK = SUBLANES * SUBLANES
HG_UNROLL = 2
VMEM_LIMIT = 56 * 1024 * 1024

BF16 = jnp.bfloat16
F32 = jnp.float32
LOG2E = math.log2(math.e)


def _dot(a, b):
    return jnp.dot(a, b, preferred_element_type=F32)


def _dot_nt(a, b):
    return lax.dot_general(a, b, (((1,), (1,)), ((), ())), preferred_element_type=F32)


def _dot_tn(a, b):
    return lax.dot_general(a, b, (((0,), (0,)), ((), ())), preferred_element_type=F32)


def _rms(x, w=None):
    y = x * lax.rsqrt(jnp.mean(x * x, axis=-1, keepdims=True) + EPS)
    return y if w is None else y * w


def _sigmoid(x):
    return 1.0 / (1.0 + jnp.exp(-x))


def _const_spec(block_shape, index_map):
    return pl.BlockSpec(block_shape, index_map, pipeline_mode=pl.Buffered(1))


def _to_slabs(slab_ref, val):
    for s in range(val.shape[1] // LANES):
        slab_ref[s] = val[:, s * LANES:(s + 1) * LANES]


def _chunk_transposed(slab_ref, rows):
    n = SUBLANES
    return jnp.concatenate(
        [jnp.concatenate([slab_ref[s, pl.ds(c * HG_CHUNK + j, n, stride=n), :]
                          for s in range(slab_ref.shape[0])], axis=1)
         for c in range(rows // HG_CHUNK) for j in range(n)], axis=0)


def _rope(z, cos, sin_signed, first_half):
    rot = jnp.where(first_half, pltpu.roll(z, 96, axis=1), pltpu.roll(z, 32, axis=1))
    return z * cos + rot * sin_signed


def _lower_bounds(lbl_ref, layer):
    out = []
    for d in range(2):
        rows = [lbl_ref[2 * l + d:2 * l + d + 1, :] for l in range(DEPTH)]
        mx = functools.reduce(jnp.maximum, rows)
        ex = [jnp.exp(r - mx) for r in rows]
        den = functools.reduce(lambda a_, b_: a_ + b_, ex)
        sm = [e / den for e in ex]
        cs = sm[0]
        for l in range(1, layer + 1):
            cs = cs + sm[l]
        out.append(cs - sm[0])
    return out


def _hgrn_gates(z, lb):
    t = jnp.exp(-jnp.abs(z))
    r = 1.0 / (1.0 + t)
    tr = t * r
    pos = z > 0
    sig_pos = jnp.where(pos, r, tr)
    sig_neg = jnp.where(pos, tr, r)
    one_m_lb = 1.0 - lb
    f = jnp.maximum(lb, LB_FLOOR) + one_m_lb * sig_pos
    return one_m_lb * sig_neg, jnp.log2(f)


def _in_proj_kernel(x_ref, nw_ref, w_ref, cos_ref, sin_ref, lbl_ref,
                    qh_ref, vh_ref, kf_ref, kb_ref, lgf_ref, lgb_ref, gg_ref,
                    q_ref, k_ref, v_ref, gate_ref, hn_ref, *, layer, tm):
    hn = _rms(x_ref[...], nw_ref[...])
    h = hn.astype(BF16)
    _to_slabs(hn_ref, hn)
    hp = _chunk_transposed(hn_ref, tm).astype(BF16)
    cw = HG_WIDTH
    lb = _lower_bounds(lbl_ref, layer)
    zq = _dot(hp, w_ref[:, 0:cw])
    qh_ref[...] = (zq * _sigmoid(zq) * (HG_DK ** -0.5)).astype(BF16)
    kf, lgf = _hgrn_gates(_dot(hp, w_ref[:, cw:2 * cw]), lb[0])
    kf_ref[...] = kf.astype(BF16)
    lgf_ref[...] = lgf
    kb, lgb = _hgrn_gates(_dot(hp, w_ref[:, 2 * cw:3 * cw]), lb[1])
    kb_ref[...] = kb.astype(BF16)
    lgb_ref[...] = lgb
    vh_ref[...] = _dot(hp, w_ref[:, 3 * cw:4 * cw]).astype(BF16)
    zg = _dot(h, w_ref[:, 4 * cw:5 * cw])
    gg_ref[...] = (zg * _sigmoid(zg)).astype(BF16)
    cos = cos_ref[...]
    sin = sin_ref[...]
    lane = lax.broadcasted_iota(jnp.int32, cos.shape, 1)
    first_half = (lane & (DA_DH - 1)) < (DA_DH // 2)
    base = 5 * HG_WIDTH
    zq = _dot(h, w_ref[:, base:base + DA_WIDTH])
    zk = _dot(h, w_ref[:, base + DA_WIDTH:base + 2 * DA_WIDTH])
    for hd in range(DA_HEADS):
        sl = slice(hd * LANES, (hd + 1) * LANES)
        q_ref[:, sl] = (_rope(zq[:, sl], cos, sin, first_half) * (LOG2E * DA_DH ** -0.5)).astype(BF16)
        k_ref[:, sl] = _rope(zk[:, sl], cos, sin, first_half).astype(BF16)
    v_ref[...] = _dot(h, w_ref[:, base + 2 * DA_WIDTH:base + 3 * DA_WIDTH]).astype(BF16)
    gbase = base + 3 * DA_WIDTH
    for c in range(4):
        zg = _dot(h, w_ref[:, gbase + c * cw:gbase + (c + 1) * cw])
        gate_ref[:, c * cw:(c + 1) * cw] = _sigmoid(zg).astype(BF16)


def _in_proj(x2d, norm_w, w_in, cos, sin, lb_logits2d, layer, seq, tm):
    T = x2d.shape[0]
    npos = seq // tm
    half = lambda dt: (pl.BlockSpec((tm, HG_WIDTH), lambda i: (i, 0)),
                       jax.ShapeDtypeStruct((T, HG_WIDTH), dt))
    outs = [half(BF16), half(BF16), half(BF16), half(BF16), half(F32), half(F32), half(BF16),
            half(BF16), half(BF16), half(BF16),
            (pl.BlockSpec((tm, 2 * D_MODEL), lambda i: (i, 0)),
             jax.ShapeDtypeStruct((T, 2 * D_MODEL), BF16))]
    return pl.pallas_call(
        functools.partial(_in_proj_kernel, layer=layer, tm=tm),
        grid=(T // tm,),
        in_specs=[
            pl.BlockSpec((tm, D_MODEL), lambda i: (i, 0)),
            _const_spec((None, 1, D_MODEL), lambda i: (layer, 0, 0)),
            _const_spec((None, D_MODEL, IN_COLS), lambda i: (layer, 0, 0)),
            pl.BlockSpec((tm, LANES), lambda i: (i % npos, 0)),
            pl.BlockSpec((tm, LANES), lambda i: (i % npos, 0)),
            _const_spec((2 * DEPTH, HG_WIDTH), lambda i: (0, 0)),
        ],
        out_specs=[o[0] for o in outs],
        out_shape=[o[1] for o in outs],
        scratch_shapes=[pltpu.VMEM((D_MODEL // LANES, tm, LANES), F32)],
        compiler_params=pltpu.CompilerParams(
            dimension_semantics=("parallel",), vmem_limit_bytes=VMEM_LIMIT),
        name="in_proj",
    )(x2d, norm_w, w_in, cos, sin, lb_logits2d)


def _hgrn_consts():
    blk = jnp.arange(SUBLANES)[:, None]
    cblk = (jnp.arange(HG_CHUNK) % SUBLANES)[None, :]
    masks = []
    for rev in (False, True):
        for m in (1, 2, 4):
            same = (blk // (2 * m)) == (cblk // (2 * m))
            r_hi = (blk % (2 * m)) >= m
            c_hi = (cblk % (2 * m)) >= m
            mk = same & (r_hi & ~c_hi if not rev else ~r_hi & c_hi)
            masks.append(mk)
    masks.append(blk == cblk)
    masks = jnp.stack(masks).astype(F32)
    grp = (jnp.arange(SUBLANES * LANES) // LANES)[:, None]
    cgrp = (jnp.arange(HG_CHUNK) // SUBLANES)[None, :]
    wsum = (grp == cgrp).astype(BF16)
    return masks, wsum


def _gather_rows(x, src, row):
    out = None
    for s in sorted({(i - src[i]) % SUBLANES for i in range(SUBLANES)}):
        rolled = x if s == 0 else pltpu.roll(x, s, axis=0)
        if out is None:
            out = rolled
        else:
            sel = functools.reduce(
                jnp.logical_or,
                [row == i for i in range(SUBLANES) if (i - src[i]) % SUBLANES == s])
            out = jnp.where(sel, rolled, out)
    return out


def _hgrn_prep(qb, kb, vb, lgc, rev, row):
    n = SUBLANES
    split = lambda a_: [a_[j * n:(j + 1) * n, :] for j in range(n)]
    q = split(qb.astype(F32))
    kk = split(kb.astype(F32))
    lg = split(lgc)
    order = list(range(n - 1, -1, -1)) if rev else list(range(n))
    w = [None] * n
    acc = None
    for j in order:
        acc = lg[j] if acc is None else acc + lg[j]
        w[j] = acc
    tot = acc
    incl = tot
    for s in (1, 2, 4):
        if not rev:
            incl = incl + jnp.where(row >= s, pltpu.roll(incl, s, axis=0), 0.0)
        else:
            incl = incl + jnp.where(row < n - s, pltpu.roll(incl, n - s, axis=0), 0.0)
    if not rev:
        excl = jnp.where(row >= 1, pltpu.roll(incl, 1, axis=0), 0.0)
        chunk_tot = jnp.broadcast_to(incl[n - 1:n, :], incl.shape)
    else:
        excl = jnp.where(row < n - 1, pltpu.roll(incl, n - 1, axis=0), 0.0)
        chunk_tot = jnp.broadcast_to(incl[0:1, :], incl.shape)
    b = [w[j] + excl for j in range(n)]

    bcat = jnp.concatenate(b, axis=0)

    c = {"vb": vb, "rev": rev}
    c["qe"] = qb * jnp.exp2(bcat).astype(BF16)

    c["lv"] = []
    for m in (1, 2, 4):
        if not rev:
            src = [(i // (2 * m)) * 2 * m + m - 1 for i in range(n)]
        else:
            src = [(i // (2 * m)) * 2 * m + m for i in range(n)]
        y = excl - _gather_rows(incl, src, row)
        e = jnp.concatenate([jnp.exp2(-jnp.abs(w[j] + y)) for j in range(n)], axis=0).astype(BF16)
        c["lv"].append((qb * e, kb * e))

    zero = jnp.zeros_like(q[0])
    rows_p = []
    for j in range(n):
        blocks = []
        for jp in range(n):
            earlier = (jp <= j) if not rev else (jp >= j)
            if not earlier:
                blocks.append(zero)
            elif jp == j:
                blocks.append(q[j] * kk[j])
            else:
                blocks.append(q[j] * kk[jp] * jnp.exp2(w[j] - w[jp]))
        rows_p.append(jnp.concatenate(blocks, axis=1))
    c["pbig"] = jnp.concatenate(rows_p, axis=0).astype(BF16)

    tot_cat = jnp.concatenate([chunk_tot] * n, axis=0)
    c["kd"] = kb * jnp.exp2(tot_cat - bcat).astype(BF16)
    c["dec"] = jnp.exp2(chunk_tot[0:1, :])
    return c


def _hgrn_mm1(c, st_ref, wsum_ref):
    st = st_ref[...]
    c["st"] = st
    c["o"] = _dot_nt(c["qe"], st.astype(BF16))
    c["am"] = [_dot_nt(qm, km) for qm, km in c["lv"]]
    c["ad"] = _dot(c["pbig"], wsum_ref[...])
    c["kv"] = _dot_tn(c["vb"], c["kd"])


def _hgrn_mid(c, masks_ref):
    n = SUBLANES
    a = c["ad"].reshape(n, n, HG_CHUNK) * masks_ref[6][None]
    for li, am in enumerate(c["am"]):
        mk = masks_ref[(3 if c["rev"] else 0) + li]
        a = a + am.reshape(n, n, HG_CHUNK) * mk[None]
    c["a"] = a.reshape(HG_CHUNK, HG_CHUNK).astype(BF16)


def _hgrn_mm2(c, st_ref):
    st_ref[...] = c["dec"] * c["st"] + c["kv"]
    return c["o"] + _dot(c["a"], c["vb"])


def _hgrn_kernel(qh_ref, vh_ref, kf_ref, kb_ref, lgf_ref, lgb_ref, gg_ref, gn_ref, masks_ref,
                 wsum_ref, out_ref, o_ref, st_ref, *, seq, heads):
    n = SUBLANES
    nchunks = seq // HG_CHUNK
    row = lax.broadcasted_iota(jnp.int32, (n, HG_DK), 0)
    st_ref[...] = jnp.zeros_like(st_ref)

    def make_body(accumulate):
        def body(it, carry):
            chains = [(hd, d) for hd in range(heads) for d in range(2)]
            ctx, bases = [], []
            for u in range(HG_UNROLL):
                c = it * HG_UNROLL + u
                fw = pl.multiple_of(c * HG_CHUNK, HG_CHUNK)
                bw = pl.multiple_of((nchunks - 1 - c) * HG_CHUNK, HG_CHUNK)
                bases.append((fw, bw))
                step = []
                for hd, d in chains:
                    rows = pl.ds(bw if d else fw, HG_CHUNK)
                    cols = slice(hd * HG_DK, (hd + 1) * HG_DK)
                    k_ref, lg_ref = (kb_ref, lgb_ref) if d else (kf_ref, lgf_ref)
                    step.append(_hgrn_prep(qh_ref[rows, cols], k_ref[rows, cols],
                                           vh_ref[rows, cols], lg_ref[rows, cols], d == 1, row))
                ctx.append(step)
            for u in range(HG_UNROLL):
                for (hd, d), cx in zip(chains, ctx[u]):
                    _hgrn_mm1(cx, st_ref.at[2 * hd + d], wsum_ref)
                for cx in ctx[u]:
                    _hgrn_mid(cx, masks_ref)
                for (hd, d), cx in zip(chains, ctx[u]):
                    o = _hgrn_mm2(cx, st_ref.at[2 * hd + d])
                    base = bases[u][d]
                    for j in range(n):
                        dst = (hd, pl.ds(base + j, n, stride=n), slice(None))
                        piece = o[j * n:(j + 1) * n, :]
                        o_ref[dst] = o_ref[dst] + piece if accumulate else piece
            return carry
        return body

    half_iters = nchunks // (2 * HG_UNROLL)
    lax.fori_loop(0, half_iters, make_body(False), 0)
    lax.fori_loop(half_iters, 2 * half_iters, make_body(True), 0)

    gn = gn_ref[...]
    rows_e = min(256, seq)

    def epilogue(c, carry):
        sl = pl.ds(pl.multiple_of(c * rows_e, rows_e), rows_e)
        for hd in range(heads):
            cols = slice(hd * HG_DK, (hd + 1) * HG_DK)
            y = _rms(o_ref[hd, sl, :], gn) * gg_ref[sl, cols].astype(F32)
            out_ref[sl, cols] = y.astype(out_ref.dtype)
        return carry

    lax.fori_loop(0, seq // rows_e, epilogue, 0)


def _hgrn(qh, vh, kf, kb, lgf, lgb, gg, gnorm, layer, heads):
    B, L, _ = qh.shape
    masks, wsum = _hgrn_consts()
    width = heads * HG_DK
    blk = pl.BlockSpec((None, L, width), lambda b, h: (b, 0, h))
    return pl.pallas_call(
        functools.partial(_hgrn_kernel, seq=L, heads=heads),
        grid=(B, HG_HEADS // heads),
        in_specs=[
            blk, blk, blk, blk, blk, blk, blk,
            _const_spec((None, 1, HG_DK), lambda b, h: (layer, 0, 0)),
            _const_spec(masks.shape, lambda b, h: (0, 0, 0)),
            _const_spec(wsum.shape, lambda b, h: (0, 0)),
        ],
        out_specs=blk,
        out_shape=jax.ShapeDtypeStruct((B, L, HG_WIDTH), BF16),
        scratch_shapes=[
            pltpu.VMEM((heads, L, HG_DK), F32),
            pltpu.VMEM((2 * heads, HG_DK, HG_DK), F32),
        ],
        compiler_params=pltpu.CompilerParams(
            dimension_semantics=("parallel", "parallel"), vmem_limit_bytes=VMEM_LIMIT),
        name="hgrn2",
    )(qh, vh, kf, kb, lgf, lgb, gg, gnorm, masks, wsum)


ATTN_SUB = 256


ATTN_SLACK = 1.0 + 2.0 ** -6
ATTN_MIN_SUM = 2.0 ** -64


def _attn_combine(p, l, lam, v, sw, lambda_init):
    sub, seq = p[0].shape
    w = [jnp.broadcast_to(sc, (sub, LANES)).astype(BF16) for sc in (1.0 / l[0], lam / l[1])]
    a = jnp.concatenate(
        [p[0][:, t * LANES:(t + 1) * LANES] * w[0] - p[1][:, t * LANES:(t + 1) * LANES] * w[1]
         for t in range(seq // LANES)], axis=1)
    return _rms(_dot(a, v), sw) * (1.0 - lambda_init)


def _attn_kernel(q_ref, k_ref, v_ref, lam_ref, sw_ref, out_ref, kaug_ref, kmax_ref,
                 *, lambda_init):
    lp = lam_ref[...]
    lam = (jnp.exp(jnp.sum(lp[0:1] * lp[1:2], axis=-1, keepdims=True))
           - jnp.exp(jnp.sum(lp[2:3] * lp[3:4], axis=-1, keepdims=True)) + lambda_init)
    sw = sw_ref[...]
    tq = q_ref.shape[0]
    seq = k_ref.shape[0]
    sub = min(ATTN_SUB, tq)
    nsub = tq // sub
    spare_lane = (DA_DH, 0)

    def own_lanes(lane, i):
        return (lane < DA_DH) if i == 0 else (lane >= DA_DH)

    @pl.when(pl.program_id(2) == 0)
    def _():
        k = k_ref[...]
        k32 = k.astype(F32)
        sq = k32 * k32
        lane = lax.broadcasted_iota(jnp.int32, k.shape, 1)
        for i in range(2):
            own = own_lanes(lane, i)
            n2 = jnp.sum(jnp.where(own, sq, 0.0), axis=-1, keepdims=True)
            kmax_ref[i] = jnp.broadcast_to(jnp.sqrt(jnp.max(n2, axis=0, keepdims=True)),
                                           kmax_ref.shape[1:])
            kaug_ref[i] = jnp.where(
                own, k32, jnp.where(lane == spare_lane[i], -1.0, 0.0)).astype(BF16)

    qa = []
    for r in range(nsub):
        q32 = q_ref[r * sub:(r + 1) * sub, :].astype(F32)
        sq = q32 * q32
        lane = lax.broadcasted_iota(jnp.int32, q32.shape, 1)
        pair = []
        for i in range(2):
            own = own_lanes(lane, i)
            qn = jnp.sqrt(jnp.sum(jnp.where(own, sq, 0.0), axis=-1, keepdims=True))
            m = qn * kmax_ref[i][0:1, 0:1] * ATTN_SLACK
            pair.append(
                jnp.where(own, q32, jnp.where(lane == spare_lane[i], m, 0.0)).astype(BF16))
        qa.append(pair)
    scores = [[_dot_nt(qa[r][i], kaug_ref[i]) for i in range(2)] for r in range(nsub)]
    lmin = None
    for r in range(nsub):
        p, l = [], []
        for i in range(2):
            e = jnp.exp2(scores[r][i])
            l.append(jnp.sum(e, axis=-1, keepdims=True))
            p.append(e.astype(BF16))
            lmin = l[i] if lmin is None else jnp.minimum(lmin, l[i])
        out_ref[r * sub:(r + 1) * sub, :] = _attn_combine(
            p, l, lam, v_ref[...], sw, lambda_init).astype(out_ref.dtype)
    shifted_ok = jnp.min(lmin) >= ATTN_MIN_SUM

    @pl.when(jnp.logical_not(shifted_ok))
    def _():
        k = k_ref[...]
        for r in range(nsub):
            q = q_ref[r * sub:(r + 1) * sub, :]
            lane = lax.broadcasted_iota(jnp.int32, q.shape, 1)
            zero = jnp.zeros_like(q)
            p, l = [], []
            for i in range(2):
                s = _dot_nt(jnp.where(own_lanes(lane, i), q, zero), k)
                e = jnp.exp2(s - jnp.max(s, axis=-1, keepdims=True))
                l.append(jnp.sum(e, axis=-1, keepdims=True))
                p.append(e.astype(BF16))
            out_ref[r * sub:(r + 1) * sub, :] = _attn_combine(
                p, l, lam, v_ref[...], sw, lambda_init).astype(out_ref.dtype)


def _attention(q3d, k3d, v3d, lam_params, subln, layer, tq):
    B, L, _ = q3d.shape
    lambda_init = 0.8 - 0.6 * math.exp(-0.3 * layer)
    return pl.pallas_call(
        functools.partial(_attn_kernel, lambda_init=lambda_init),
        grid=(B, DA_HEADS, L // tq),
        in_specs=[
            pl.BlockSpec((None, tq, DA_DV), lambda b, h, i: (b, i, h)),
            pl.BlockSpec((None, L, DA_DV), lambda b, h, i: (b, 0, h)),
            pl.BlockSpec((None, L, DA_DV), lambda b, h, i: (b, 0, h)),
            _const_spec((None, 4, DA_DH), lambda b, h, i: (layer, 0, 0)),
            _const_spec((None, 1, DA_DV), lambda b, h, i: (layer, 0, 0)),
        ],
        out_specs=pl.BlockSpec((None, tq, DA_DV), lambda b, h, i: (b, i, h)),
        out_shape=jax.ShapeDtypeStruct((B, L, DA_WIDTH), BF16),
        scratch_shapes=[
            pltpu.VMEM((2, L, DA_DV), BF16),
            pltpu.VMEM((2, SUBLANES, LANES), F32),
        ],
        compiler_params=pltpu.CompilerParams(
            dimension_semantics=("parallel", "parallel", "arbitrary"),
            vmem_limit_bytes=VMEM_LIMIT),
        name="diff_attn",
    )(q3d, k3d, v3d, lam_params, subln)


def _mix_kernel(x_ref, a_ref, b_ref, gate_ref, wa_ref, wb_ref, wo_ref, nw_ref, out_ref):
    ga = gate_ref[:, :D_MODEL].astype(F32)
    gb = gate_ref[:, D_MODEL:].astype(F32)
    m = ga * _dot(a_ref[...], wa_ref[...]) + gb * _dot(b_ref[...], wb_ref[...])
    r = _dot(m.astype(BF16), wo_ref[...])
    out_ref[...] = x_ref[...] + _rms(r, nw_ref[...])


def _mix(x2d, a2d, b2d, gate2d, wa, wb, wo, norm_w, layer, tm):
    T = x2d.shape[0]
    return pl.pallas_call(
        _mix_kernel,
        grid=(T // tm,),
        in_specs=[
            pl.BlockSpec((tm, D_MODEL), lambda i: (i, 0)),
            pl.BlockSpec((tm, HG_WIDTH), lambda i: (i, 0)),
            pl.BlockSpec((tm, DA_WIDTH), lambda i: (i, 0)),
            pl.BlockSpec((tm, 2 * D_MODEL), lambda i: (i, 0)),
            _const_spec((None, HG_WIDTH, D_MODEL), lambda i: (layer, 0, 0)),
            _const_spec((None, DA_WIDTH, D_MODEL), lambda i: (layer, 0, 0)),
            _const_spec((None, D_MODEL, D_MODEL), lambda i: (layer, 0, 0)),
            _const_spec((None, 1, D_MODEL), lambda i: (layer, 0, 0)),
        ],
        out_specs=pl.BlockSpec((tm, D_MODEL), lambda i: (i, 0)),
        out_shape=jax.ShapeDtypeStruct((T, D_MODEL), F32),
        compiler_params=pltpu.CompilerParams(
            dimension_semantics=("parallel",), vmem_limit_bytes=VMEM_LIMIT),
        name="mix_out",
    )(x2d, a2d, b2d, gate2d, wa, wb, wo, norm_w)


FFN_HALO = 8
FFN_TF = 256


GELU_C = math.sqrt(2.0 / math.pi)
GELU_A = 0.044715


def _ffn_kernel(x_ref, xp_ref, xn_ref, p_ref, nwpre_ref, wup_ref, cw_ref, cb_ref, wdn_ref,
                nwpost_ref, wple_ref, wpg_ref, nwple_ref, out_ref, h_ref, slab_ref, act_ref,
                *, tm, tiles_per_seq):
    i = pl.program_id(0)
    pos = i % tiles_per_seq
    nw = nwpre_ref[...]
    x = x_ref[...]
    n = SUBLANES
    nch = tm // HG_CHUNK
    _to_slabs(slab_ref, _rms(x, nw))
    h_ref[0:tm, :] = _chunk_transposed(slab_ref, tm).astype(BF16)
    hp = _rms(xp_ref[...], nw) * jnp.where(pos == 0, 0.0, 1.0)
    hn = _rms(xn_ref[...], nw) * jnp.where(pos == tiles_per_seq - 1, 0.0, 1.0)
    h_ref[tm:, :] = jnp.concatenate([hp, hn], axis=0).astype(BF16)
    row = lax.broadcasted_iota(jnp.int32, (nch, n, FFN_TF), 1)

    def conv(u, col, scale):
        body = u[0:tm].reshape(nch, n, n, FFN_TF)
        before = u[tm:tm + n]
        after = u[tm + n:tm + 2 * n]
        first, last = body[:, 0], body[:, n - 1]
        prevs = jnp.concatenate([before[None], last[:-1]], axis=0)
        nexts = jnp.concatenate([first[1:], after[None]], axis=0)
        dn0 = jnp.where(row == 0, pltpu.roll(prevs, 1, axis=1), pltpu.roll(last, 1, axis=1))
        up7 = jnp.where(row == n - 1, pltpu.roll(nexts, n - 1, axis=1),
                        pltpu.roll(first, n - 1, axis=1))
        dn = jnp.concatenate([dn0[:, None], body[:, :n - 1]], axis=1)
        up = jnp.concatenate([body[:, 1:], up7[:, None]], axis=1)
        cw = cw_ref[:, col] * scale
        return dn * cw[0:1] + body * cw[1:2] + up * cw[2:3] + cb_ref[:, col] * scale

    for c in range(D_FF // FFN_TF):
        gcol = slice(c * FFN_TF, (c + 1) * FFN_TF)
        vcol = slice(D_FF + c * FFN_TF, D_FF + (c + 1) * FFN_TF)
        h = h_ref[...]
        g = conv(_dot(h, wup_ref[:, gcol]), gcol, 1.0)
        half_val = conv(_dot(h, wup_ref[:, vcol]), vcol, 0.5)
        t = jnp.tanh(g * (GELU_C + (GELU_C * GELU_A) * (g * g)))
        act_ref[:, gcol] = (g * (1.0 + t) * half_val).reshape(tm, FFN_TF).astype(BF16)

    _to_slabs(slab_ref, _dot(act_ref[...], wdn_ref[...]))
    x2 = x + _rms(_chunk_transposed(slab_ref, tm), nwpost_ref[...])
    e = _dot(p_ref[...].astype(BF16), wple_ref[...])
    g = _sigmoid(_dot(_rms(x2).astype(BF16), wpg_ref[...]))
    out_ref[...] = x2 + _rms(g * e, nwple_ref[...])


def _ffn_ple(x2d, p2d, nw_pre, w_up, conv_w, conv_b, w_down, nw_post,
             w_ple, w_ple_gate, nw_ple, layer, seq, tm):
    T = x2d.shape[0]
    tiles_per_seq = seq // tm
    hb = tm // FFN_HALO
    nhalo = T // FFN_HALO

    def lw(shape):
        return _const_spec((None,) + shape, lambda i: (layer,) + (0,) * len(shape))

    return pl.pallas_call(
        functools.partial(_ffn_kernel, tm=tm, tiles_per_seq=tiles_per_seq),
        grid=(T // tm,),
        in_specs=[
            pl.BlockSpec((tm, D_MODEL), lambda i: (i, 0)),
            pl.BlockSpec((FFN_HALO, D_MODEL), lambda i: (jnp.maximum(i * hb - 1, 0), 0)),
            pl.BlockSpec((FFN_HALO, D_MODEL), lambda i: (jnp.minimum((i + 1) * hb, nhalo - 1), 0)),
            pl.BlockSpec((None, tm, PLE_DIM), lambda i: (layer, i, 0)),
            lw((1, D_MODEL)),
            lw((D_MODEL, 2 * D_FF)),
            lw((3, 2 * D_FF)),
            lw((1, 2 * D_FF)),
            lw((D_FF, D_MODEL)),
            lw((1, D_MODEL)),
            lw((PLE_DIM, D_MODEL)),
            lw((D_MODEL, D_MODEL)),
            lw((1, D_MODEL)),
        ],
        out_specs=pl.BlockSpec((tm, D_MODEL), lambda i: (i, 0)),
        out_shape=jax.ShapeDtypeStruct((T, D_MODEL), F32),
        scratch_shapes=[
            pltpu.VMEM((tm + 2 * FFN_HALO, D_MODEL), BF16),
            pltpu.VMEM((D_MODEL // LANES, tm, LANES), F32),
            pltpu.VMEM((tm, D_FF), BF16),
        ],
        compiler_params=pltpu.CompilerParams(
            dimension_semantics=("parallel",), vmem_limit_bytes=VMEM_LIMIT),
        name="convffn_ple",
    )(x2d, x2d, x2d, p2d, nw_pre, w_up, conv_w, conv_b, w_down, nw_post,
      w_ple, w_ple_gate, nw_ple)


def _rope_tables(L):
    inv = 1.0 / (ROPE_THETA ** (jnp.arange(0, DA_DH, 2, dtype=F32) / DA_DH))
    ang = jnp.arange(L, dtype=F32)[:, None] * inv[None, :]
    cos = jnp.tile(jnp.cos(ang), (1, 4))
    sin = jnp.sin(ang)
    sin_signed = jnp.tile(jnp.concatenate([-sin, sin], axis=-1), (1, 2))
    return cos, sin_signed


def _trunk(x, p, w, tm_in, tm_mix, tm_ffn, tq, hg_heads=2):
    B, L, _ = x.shape
    T = B * L
    cos, sin = _rope_tables(L)
    x2d = x.reshape(T, D_MODEL)
    p3d = p.reshape(DEPTH, T, PLE_DIM)
    for l in range(DEPTH):
        (qh, vh, kf, kb, lgf, lgb, gg, q, k, v, gate) = _in_proj(
            x2d, w["norm_mix_pre"], w["w_in"], cos, sin, w["lb_logits"], l, L, tm_in)
        seq3 = lambda t: t.reshape(B, L, HG_WIDTH)
        a = _hgrn(seq3(qh), seq3(vh), seq3(kf), seq3(kb), seq3(lgf), seq3(lgb), seq3(gg),
                  w["hgrn_gnorm"], l, hg_heads)
        bb = _attention(q.reshape(B, L, DA_WIDTH), k.reshape(B, L, DA_WIDTH),
                        v.reshape(B, L, DA_WIDTH), w["diff_lambda"], w["diff_subln"], l, tq)
        x2d = _mix(x2d, a.reshape(T, HG_WIDTH), bb.reshape(T, DA_WIDTH), gate,
                   w["w_branch_a"], w["w_branch_b"], w["w_out"], w["norm_mix_post"], l, tm_mix)
        x2d = _ffn_ple(x2d, p3d, w["norm_ffn_pre"], w["w_up"], w["conv_w"], w["conv_b"],
                       w["w_down"], w["norm_ffn_post"], w["w_ple"], w["w_ple_gate"],
                       w["norm_ple"], l, L, tm_ffn)
    return x2d.reshape(B, L, D_MODEL)


def _prep_weights(w_in, hgrn_lb_logits, hgrn_gnorm, diff_lambda, diff_subln, w_branch_a,
                  w_branch_b, w_out, norm_mix_pre, norm_mix_post, w_up, conv_w, conv_b, w_down,
                  norm_ffn_pre, norm_ffn_post, w_ple, w_ple_gate, norm_ple):
    def vec(a):
        return a.reshape(DEPTH, 1, a.shape[-1]).astype(F32)

    return {
        "w_in": w_in.astype(BF16),
        "lb_logits": hgrn_lb_logits.reshape(2 * DEPTH, HG_WIDTH).astype(F32),
        "hgrn_gnorm": vec(hgrn_gnorm),
        "diff_lambda": diff_lambda.astype(F32),
        "diff_subln": vec(diff_subln),
        "w_branch_a": w_branch_a.astype(BF16),
        "w_branch_b": w_branch_b.astype(BF16),
        "w_out": w_out.astype(BF16),
        "norm_mix_pre": vec(norm_mix_pre),
        "norm_mix_post": vec(norm_mix_post),
        "w_up": w_up.astype(BF16),
        "conv_w": conv_w.astype(F32),
        "conv_b": vec(conv_b),
        "w_down": w_down.astype(BF16),
        "norm_ffn_pre": vec(norm_ffn_pre),
        "norm_ffn_post": vec(norm_ffn_post),
        "w_ple": w_ple.astype(BF16),
        "w_ple_gate": w_ple_gate.astype(BF16),
        "norm_ple": vec(norm_ple),
    }


def _tile(n, pref):
    t = min(n, pref)
    assert n % t == 0
    return t


def kernel(x_prompt, x_sample, p_prompt, p_sample, w_in, hgrn_lb_logits, hgrn_gnorm, diff_lambda, diff_subln, w_branch_a, w_branch_b, w_out, norm_mix_pre, norm_mix_post, w_up, conv_w, conv_b, w_down, norm_ffn_pre, norm_ffn_post, w_ple, w_ple_gate, norm_ple):
    w = _prep_weights(w_in, hgrn_lb_logits, hgrn_gnorm, diff_lambda, diff_subln, w_branch_a,
                      w_branch_b, w_out, norm_mix_pre, norm_mix_post, w_up, conv_w, conv_b,
                      w_down, norm_ffn_pre, norm_ffn_post, w_ple, w_ple_gate, norm_ple)
    outs = []
    for x, p in ((x_prompt, p_prompt), (x_sample, p_sample)):
        L = x.shape[1]
        outs.append(_trunk(x, p, w, tm_in=_tile(L, 512), tm_mix=_tile(L, 512),
                           tm_ffn=_tile(L, 512), tq=_tile(L, 512)))
    return tuple(outs)
```

```python
import functools
import math

import jax
import jax.numpy as jnp
from jax import lax
from jax.experimental import pallas as pl
from jax.experimental.pallas import tpu as pltpu

D_MODEL = 1024
DEPTH = 4
PLE_DIM = 256
HG_WIDTH = D_MODEL // 2
HG_HEADS = 4
HG_DK = HG_WIDTH // HG_HEADS
LB_FLOOR = 1e-30
DA_WIDTH = D_MODEL // 2
DA_HEADS = 4
DA_DV = DA_WIDTH // DA_HEADS
DA_DH = DA_DV // 2
D_FF = 2816
ROPE_THETA = 10000.0
EPS = 1e-6
IN_COLS = 5 * HG_WIDTH + 3 * DA_WIDTH + 2 * D_MODEL

LANES = 128
SUBLANES = 8
HG_CHUNK = SUBLANES * SUBLANES
HG_CHAINS = 16
VMEM_LIMIT = 56 * 1024 * 1024

BF16 = jnp.bfloat16
F32 = jnp.float32
LOG2E = math.log2(math.e)


def _dot(a, b):
    return jnp.dot(a, b, preferred_element_type=F32)


def _dot_nt(a, b):
    return lax.dot_general(a, b, (((1,), (1,)), ((), ())), preferred_element_type=F32)


def _dot_tn(a, b):
    return lax.dot_general(a, b, (((0,), (0,)), ((), ())), preferred_element_type=F32)


def _rms(x, w=None):
    y = x * lax.rsqrt(jnp.mean(x * x, axis=-1, keepdims=True) + EPS)
    return y if w is None else y * w


def _sigmoid(x):
    return 1.0 / (1.0 + jnp.exp(-x))


def _const_spec(block_shape, index_map):
    return pl.BlockSpec(block_shape, index_map, pipeline_mode=pl.Buffered(1))


def _to_slabs(slab_ref, val, row0=0):
    for s in range(val.shape[1] // LANES):
        slab_ref[s, row0:row0 + val.shape[0], :] = val[:, s * LANES:(s + 1) * LANES]


def _chunk_transposed(slab_ref, rows, row0=0):
    n = SUBLANES
    return jnp.concatenate(
        [jnp.concatenate([slab_ref[s, pl.ds(row0 + c * HG_CHUNK + j, n, stride=n), :]
                          for s in range(slab_ref.shape[0])], axis=1)
         for c in range(rows // HG_CHUNK) for j in range(n)], axis=0)


def _rope(z, cos, sin_signed, first_half):
    rot = jnp.where(first_half, pltpu.roll(z, 96, axis=1), pltpu.roll(z, 32, axis=1))
    return z * cos + rot * sin_signed


def _lower_bounds(lbl_ref, layer):
    out = []
    for d in range(2):
        rows = [lbl_ref[2 * l + d:2 * l + d + 1, :] for l in range(DEPTH)]
        mx = functools.reduce(jnp.maximum, rows)
        ex = [jnp.exp(r - mx) for r in rows]
        den = functools.reduce(lambda a_, b_: a_ + b_, ex)
        sm = [e / den for e in ex]
        cs = sm[0]
        for l in range(1, layer + 1):
            cs = cs + sm[l]
        out.append(cs - sm[0])
    return out


def _hgrn_gates(z, lb):
    t = jnp.exp(-jnp.abs(z))
    r = 1.0 / (1.0 + t)
    tr = t * r
    pos = z > 0
    sig_pos = jnp.where(pos, r, tr)
    sig_neg = jnp.where(pos, tr, r)
    one_m_lb = 1.0 - lb
    f = jnp.maximum(lb, LB_FLOOR) + one_m_lb * sig_pos
    return one_m_lb * sig_neg, jnp.log2(f)


def _in_proj_kernel(x_ref, nw_ref, w_ref, cos_ref, sin_ref, lbl_ref,
                    qh_ref, vh_ref, kf_ref, kb_ref, lgf_ref, lgb_ref, gg_ref,
                    q_ref, k_ref, v_ref, gate_ref, hn_ref, *, layer, tm):
    hn = _rms(x_ref[...], nw_ref[...])
    h = hn.astype(BF16)
    _to_slabs(hn_ref, hn)
    hp = _chunk_transposed(hn_ref, tm).astype(BF16)
    cw = HG_WIDTH
    lb = _lower_bounds(lbl_ref, layer)
    cos = cos_ref[...]
    sin = sin_ref[...]
    lane = lax.broadcasted_iota(jnp.int32, cos.shape, 1)
    first_half = (lane & (DA_DH - 1)) < (DA_DH // 2)
    base = 5 * HG_WIDTH
    gbase = base + 3 * DA_WIDTH

    def col(lhs, c):
        return _dot(lhs, w_ref[:, c * cw:(c + 1) * cw])

    def merge_gate(c):
        gate_ref[:, c * cw:(c + 1) * cw] = col(h, gbase // cw + c).astype(BF16)

    def rope_out(out_ref, z, scale):
        for hd in range(DA_HEADS):
            sl = slice(hd * LANES, (hd + 1) * LANES)
            out_ref[:, sl] = (_rope(z[:, sl], cos, sin, first_half) * scale).astype(BF16)

    zq = col(hp, 0)
    qh_ref[...] = (zq * _sigmoid(zq) * (HG_DK ** -0.5)).astype(BF16)
    kf, lgf = _hgrn_gates(col(hp, 1), lb[0])
    kf_ref[...] = kf.astype(BF16)
    lgf_ref[...] = lgf
    kb, lgb = _hgrn_gates(col(hp, 2), lb[1])
    kb_ref[...] = kb.astype(BF16)
    lgb_ref[...] = lgb
    vh_ref[...] = col(hp, 3).astype(BF16)
    zg = col(h, 4)
    gg_ref[...] = (zg * _sigmoid(zg)).astype(BF16)
    zq_d = col(h, base // cw)
    zk_d = col(h, base // cw + 1)
    rope_out(q_ref, zq_d, LOG2E * DA_DH ** -0.5)
    rope_out(k_ref, zk_d, 1.0)
    v_ref[...] = col(h, base // cw + 2).astype(BF16)
    for c in range(4):
        merge_gate(c)


def _in_proj(x2d, norm_w, w_in, cos, sin, lb_logits2d, layer, seq, tm):
    T = x2d.shape[0]
    npos = seq // tm
    half = lambda dt: (pl.BlockSpec((tm, HG_WIDTH), lambda i: (i, 0)),
                       jax.ShapeDtypeStruct((T, HG_WIDTH), dt))
    outs = [half(BF16), half(BF16), half(BF16), half(BF16), half(F32), half(F32), half(BF16),
            half(BF16), half(BF16), half(BF16),
            (pl.BlockSpec((tm, 2 * D_MODEL), lambda i: (i, 0)),
             jax.ShapeDtypeStruct((T, 2 * D_MODEL), BF16))]
    return pl.pallas_call(
        functools.partial(_in_proj_kernel, layer=layer, tm=tm),
        grid=(T // tm,),
        in_specs=[
            pl.BlockSpec((tm, D_MODEL), lambda i: (i, 0)),
            _const_spec((None, 1, D_MODEL), lambda i: (layer, 0, 0)),
            _const_spec((None, D_MODEL, IN_COLS), lambda i: (layer, 0, 0)),
            pl.BlockSpec((tm, LANES), lambda i: (i % npos, 0)),
            pl.BlockSpec((tm, LANES), lambda i: (i % npos, 0)),
            _const_spec((2 * DEPTH, HG_WIDTH), lambda i: (0, 0)),
        ],
        out_specs=[o[0] for o in outs],
        out_shape=[o[1] for o in outs],
        scratch_shapes=[pltpu.VMEM((D_MODEL // LANES, tm, LANES), F32)],
        compiler_params=pltpu.CompilerParams(
            dimension_semantics=("parallel",), vmem_limit_bytes=VMEM_LIMIT),
        name="in_proj",
    )(x2d, norm_w, w_in, cos, sin, lb_logits2d)


def _hgrn_consts():
    blk = jnp.arange(SUBLANES)[:, None]
    cblk = (jnp.arange(HG_CHUNK) % SUBLANES)[None, :]
    masks = []
    for rev in (False, True):
        for m in (1, 2, 4):
            same = (blk // (2 * m)) == (cblk // (2 * m))
            r_hi = (blk % (2 * m)) >= m
            c_hi = (cblk % (2 * m)) >= m
            mk = same & (r_hi & ~c_hi if not rev else ~r_hi & c_hi)
            masks.append(mk)
    masks.append(blk == cblk)
    masks = jnp.stack(masks).astype(F32)
    grp = (jnp.arange(SUBLANES * LANES) // LANES)[:, None]
    cgrp = (jnp.arange(HG_CHUNK) // SUBLANES)[None, :]
    wsum = (grp == cgrp).astype(BF16)
    return masks, wsum


def _gather_rows(x, src, row):
    out = None
    for s in sorted({(i - src[i]) % SUBLANES for i in range(SUBLANES)}):
        rolled = x if s == 0 else pltpu.roll(x, s, axis=0)
        if out is None:
            out = rolled
        else:
            sel = functools.reduce(
                jnp.logical_or,
                [row == i for i in range(SUBLANES) if (i - src[i]) % SUBLANES == s])
            out = jnp.where(sel, rolled, out)
    return out


def _hgrn_prep(qb, kb, vb, lgc, rev, row):
    n = SUBLANES
    split = lambda a_: [a_[j * n:(j + 1) * n, :] for j in range(n)]
    q = split(qb.astype(F32))
    kk = split(kb.astype(F32))
    lg = split(lgc)
    order = list(range(n - 1, -1, -1)) if rev else list(range(n))
    w = [None] * n
    acc = None
    for j in order:
        acc = lg[j] if acc is None else acc + lg[j]
        w[j] = acc
    tot = acc
    incl = tot
    for s in (1, 2, 4):
        if not rev:
            incl = incl + jnp.where(row >= s, pltpu.roll(incl, s, axis=0), 0.0)
        else:
            incl = incl + jnp.where(row < n - s, pltpu.roll(incl, n - s, axis=0), 0.0)
    if not rev:
        excl = jnp.where(row >= 1, pltpu.roll(incl, 1, axis=0), 0.0)
        chunk_tot = jnp.broadcast_to(incl[n - 1:n, :], incl.shape)
    else:
        excl = jnp.where(row < n - 1, pltpu.roll(incl, n - 1, axis=0), 0.0)
        chunk_tot = jnp.broadcast_to(incl[0:1, :], incl.shape)
    b = [w[j] + excl for j in range(n)]

    bcat = jnp.concatenate(b, axis=0)

    c = {"vb": vb, "rev": rev}
    c["qe"] = qb * jnp.exp2(bcat).astype(BF16)

    c["lv"] = []
    for m in (1, 2, 4):
        if not rev:
            src = [(i // (2 * m)) * 2 * m + m - 1 for i in range(n)]
        else:
            src = [(i // (2 * m)) * 2 * m + m for i in range(n)]
        y = excl - _gather_rows(incl, src, row)
        e = jnp.concatenate([jnp.exp2(-jnp.abs(w[j] + y)) for j in range(n)], axis=0).astype(BF16)
        c["lv"].append((qb * e, kb * e))

    zero = jnp.zeros_like(q[0])
    rows_p = []
    for j in range(n):
        blocks = []
        for jp in range(n):
            earlier = (jp <= j) if not rev else (jp >= j)
            if not earlier:
                blocks.append(zero)
            elif jp == j:
                blocks.append(q[j] * kk[j])
            else:
                blocks.append(q[j] * kk[jp] * jnp.exp2(w[j] - w[jp]))
        rows_p.append(jnp.concatenate(blocks, axis=1))
    c["pbig"] = jnp.concatenate(rows_p, axis=0).astype(BF16)

    tot_cat = jnp.concatenate([chunk_tot] * n, axis=0)
    c["kd"] = kb * jnp.exp2(tot_cat - bcat).astype(BF16)
    c["dec"] = jnp.exp2(chunk_tot[0:1, :])
    return c


def _hgrn_mm1(c, st_ref, wsum_ref):
    st = st_ref[...]
    c["st"] = st
    c["o"] = _dot_nt(c["qe"], st.astype(BF16))
    c["am"] = [_dot_nt(qm, km) for qm, km in c["lv"]]
    c["ad"] = _dot(c["pbig"], wsum_ref[...])
    c["kv"] = _dot_tn(c["vb"], c["kd"])


def _hgrn_mid(c, masks_ref):
    n = SUBLANES
    a = c["ad"].reshape(n, n, HG_CHUNK) * masks_ref[6][None]
    for li, am in enumerate(c["am"]):
        mk = masks_ref[(3 if c["rev"] else 0) + li]
        a = a + am.reshape(n, n, HG_CHUNK) * mk[None]
    c["a"] = a.reshape(HG_CHUNK, HG_CHUNK).astype(BF16)


def _hgrn_mm2(c, st_ref):
    st_ref[...] = c["dec"] * c["st"] + c["kv"]
    return c["o"] + _dot(c["a"], c["vb"])


def _hgrn_kernel(qh_ref, vh_ref, kf_ref, kb_ref, lgf_ref, lgb_ref, gg_ref, gn_ref, masks_ref,
                 wsum_ref, out_ref, o_ref, st_ref, *, seq, heads):
    n = SUBLANES
    nchunks = seq // HG_CHUNK
    unroll = max(1, HG_CHAINS // (2 * heads))
    while (nchunks // 2) % unroll:
        unroll //= 2
    row = lax.broadcasted_iota(jnp.int32, (n, HG_DK), 0)
    st_ref[...] = jnp.zeros_like(st_ref)

    def make_body(accumulate):
        def body(it, carry):
            chains = [(hd, d) for hd in range(heads) for d in range(2)]
            ctx, bases = [], []
            for u in range(unroll):
                c = it * unroll + u
                fw = pl.multiple_of(c * HG_CHUNK, HG_CHUNK)
                bw = pl.multiple_of((nchunks - 1 - c) * HG_CHUNK, HG_CHUNK)
                bases.append((fw, bw))
                step = []
                for hd, d in chains:
                    rows = pl.ds(bw if d else fw, HG_CHUNK)
                    cols = slice(hd * HG_DK, (hd + 1) * HG_DK)
                    k_ref, lg_ref = (kb_ref, lgb_ref) if d else (kf_ref, lgf_ref)
                    step.append(_hgrn_prep(qh_ref[rows, cols], k_ref[rows, cols],
                                           vh_ref[rows, cols], lg_ref[rows, cols], d == 1, row))
                ctx.append(step)
            for u in range(unroll):
                for (hd, d), cx in zip(chains, ctx[u]):
                    _hgrn_mm1(cx, st_ref.at[2 * hd + d], wsum_ref)
                for cx in ctx[u]:
                    _hgrn_mid(cx, masks_ref)
                for (hd, d), cx in zip(chains, ctx[u]):
                    o = _hgrn_mm2(cx, st_ref.at[2 * hd + d])
                    base = bases[u][d]
                    for j in range(n):
                        dst = (hd, pl.ds(base + j, n, stride=n), slice(None))
                        piece = o[j * n:(j + 1) * n, :]
                        o_ref[dst] = o_ref[dst] + piece if accumulate else piece
            return carry
        return body

    half_iters = nchunks // (2 * unroll)
    lax.fori_loop(0, half_iters, make_body(False), 0)
    lax.fori_loop(half_iters, 2 * half_iters, make_body(True), 0)

    gn = gn_ref[...]
    rows_e = min(256, seq)

    def epilogue(c, carry):
        sl = pl.ds(pl.multiple_of(c * rows_e, rows_e), rows_e)
        for hd in range(heads):
            cols = slice(hd * HG_DK, (hd + 1) * HG_DK)
            y = _rms(o_ref[hd, sl, :], gn) * gg_ref[sl, cols].astype(F32)
            out_ref[sl, cols] = y.astype(out_ref.dtype)
        return carry

    lax.fori_loop(0, seq // rows_e, epilogue, 0)


def _hgrn(qh, vh, kf, kb, lgf, lgb, gg, gnorm, layer, heads):
    B, L, _ = qh.shape
    masks, wsum = _hgrn_consts()
    width = heads * HG_DK
    blk = pl.BlockSpec((None, L, width), lambda b, h: (b, 0, h))
    return pl.pallas_call(
        functools.partial(_hgrn_kernel, seq=L, heads=heads),
        grid=(B, HG_HEADS // heads),
        in_specs=[
            blk, blk, blk, blk, blk, blk, blk,
            _const_spec((None, 1, HG_DK), lambda b, h: (layer, 0, 0)),
            _const_spec(masks.shape, lambda b, h: (0, 0, 0)),
            _const_spec(wsum.shape, lambda b, h: (0, 0)),
        ],
        out_specs=blk,
        out_shape=jax.ShapeDtypeStruct((B, L, HG_WIDTH), BF16),
        scratch_shapes=[
            pltpu.VMEM((heads, L, HG_DK), F32),
            pltpu.VMEM((2 * heads, HG_DK, HG_DK), F32),
        ],
        compiler_params=pltpu.CompilerParams(
            dimension_semantics=("parallel", "parallel"), vmem_limit_bytes=VMEM_LIMIT),
        name="hgrn2",
    )(qh, vh, kf, kb, lgf, lgb, gg, gnorm, masks, wsum)


ATTN_SUB = 128


ATTN_SLACK = 1.0 + 2.0 ** -6
ATTN_MIN_SUM = 2.0 ** -64


def _attn_combine(p, l, lam, v, sw, lambda_init):
    sub, seq = p[0].shape
    w = [jnp.broadcast_to(sc, (sub, LANES)).astype(BF16) for sc in (1.0 / l[0], lam / l[1])]
    a = jnp.concatenate(
        [p[0][:, t * LANES:(t + 1) * LANES] * w[0] - p[1][:, t * LANES:(t + 1) * LANES] * w[1]
         for t in range(seq // LANES)], axis=1)
    return _rms(_dot(a, v), sw) * (1.0 - lambda_init)


def _attn_kernel(q_ref, k_ref, v_ref, lam_ref, sw_ref, out_ref, kaug_ref, kmax_ref,
                 *, lambda_init):
    lp = lam_ref[...]
    lam = (jnp.exp(jnp.sum(lp[0:1] * lp[1:2], axis=-1, keepdims=True))
           - jnp.exp(jnp.sum(lp[2:3] * lp[3:4], axis=-1, keepdims=True)) + lambda_init)
    sw = sw_ref[...]
    tq = q_ref.shape[0]
    seq = k_ref.shape[0]
    sub = min(ATTN_SUB, tq)
    nsub = tq // sub
    spare_lane = (DA_DH, 0)

    def own_lanes(lane, i):
        return (lane < DA_DH) if i == 0 else (lane >= DA_DH)

    @pl.when(pl.program_id(2) == 0)
    def _():
        k = k_ref[...]
        k32 = k.astype(F32)
        sq = k32 * k32
        lane = lax.broadcasted_iota(jnp.int32, k.shape, 1)
        for i in range(2):
            own = own_lanes(lane, i)
            n2 = jnp.sum(jnp.where(own, sq, 0.0), axis=-1, keepdims=True)
            kmax_ref[i] = jnp.broadcast_to(jnp.sqrt(jnp.max(n2, axis=0, keepdims=True)),
                                           kmax_ref.shape[1:])
            kaug_ref[i] = jnp.where(
                own, k32, jnp.where(lane == spare_lane[i], -1.0, 0.0)).astype(BF16)

    qa = []
    for r in range(nsub):
        q32 = q_ref[r * sub:(r + 1) * sub, :].astype(F32)
        sq = q32 * q32
        lane = lax.broadcasted_iota(jnp.int32, q32.shape, 1)
        pair = []
        for i in range(2):
            own = own_lanes(lane, i)
            qn = jnp.sqrt(jnp.sum(jnp.where(own, sq, 0.0), axis=-1, keepdims=True))
            m = qn * kmax_ref[i][0:1, 0:1] * ATTN_SLACK
            pair.append(
                jnp.where(own, q32, jnp.where(lane == spare_lane[i], m, 0.0)).astype(BF16))
        qa.append(pair)
    scores = [[_dot_nt(qa[r][i], kaug_ref[i]) for i in range(2)] for r in range(nsub)]
    lmin = None
    for r in range(nsub):
        p, l = [], []
        for i in range(2):
            e = jnp.exp2(scores[r][i])
            l.append(jnp.sum(e, axis=-1, keepdims=True))
            p.append(e.astype(BF16))
            lmin = l[i] if lmin is None else jnp.minimum(lmin, l[i])
        out_ref[r * sub:(r + 1) * sub, :] = _attn_combine(
            p, l, lam, v_ref[...], sw, lambda_init).astype(out_ref.dtype)
    shifted_ok = jnp.min(lmin) >= ATTN_MIN_SUM

    @pl.when(jnp.logical_not(shifted_ok))
    def _():
        k = k_ref[...]
        for r in range(nsub):
            q = q_ref[r * sub:(r + 1) * sub, :]
            lane = lax.broadcasted_iota(jnp.int32, q.shape, 1)
            zero = jnp.zeros_like(q)
            p, l = [], []
            for i in range(2):
                s = _dot_nt(jnp.where(own_lanes(lane, i), q, zero), k)
                e = jnp.exp2(s - jnp.max(s, axis=-1, keepdims=True))
                l.append(jnp.sum(e, axis=-1, keepdims=True))
                p.append(e.astype(BF16))
            out_ref[r * sub:(r + 1) * sub, :] = _attn_combine(
                p, l, lam, v_ref[...], sw, lambda_init).astype(out_ref.dtype)


def _attention(q3d, k3d, v3d, lam_params, subln, layer, tq):
    B, L, _ = q3d.shape
    lambda_init = 0.8 - 0.6 * math.exp(-0.3 * layer)
    return pl.pallas_call(
        functools.partial(_attn_kernel, lambda_init=lambda_init),
        grid=(B, DA_HEADS, L // tq),
        in_specs=[
            pl.BlockSpec((None, tq, DA_DV), lambda b, h, i: (b, i, h)),
            pl.BlockSpec((None, L, DA_DV), lambda b, h, i: (b, 0, h)),
            pl.BlockSpec((None, L, DA_DV), lambda b, h, i: (b, 0, h)),
            _const_spec((None, 4, DA_DH), lambda b, h, i: (layer, 0, 0)),
            _const_spec((None, 1, DA_DV), lambda b, h, i: (layer, 0, 0)),
        ],
        out_specs=pl.BlockSpec((None, tq, DA_DV), lambda b, h, i: (b, i, h)),
        out_shape=jax.ShapeDtypeStruct((B, L, DA_WIDTH), BF16),
        scratch_shapes=[
            pltpu.VMEM((2, L, DA_DV), BF16),
            pltpu.VMEM((2, SUBLANES, LANES), F32),
        ],
        compiler_params=pltpu.CompilerParams(
            dimension_semantics=("parallel", "parallel", "arbitrary"),
            vmem_limit_bytes=VMEM_LIMIT),
        name="diff_attn",
    )(q3d, k3d, v3d, lam_params, subln)


def _mix_kernel(x_ref, a_ref, b_ref, gate_ref, wa_ref, wb_ref, wo_ref, nw_ref, out_ref):
    ga = _sigmoid(gate_ref[:, :D_MODEL].astype(F32))
    gb = _sigmoid(gate_ref[:, D_MODEL:].astype(F32))
    m = ga * _dot(a_ref[...], wa_ref[...]) + gb * _dot(b_ref[...], wb_ref[...])
    r = _dot(m.astype(BF16), wo_ref[...])
    out_ref[...] = x_ref[...] + _rms(r, nw_ref[...])


def _mix(x2d, a2d, b2d, gate2d, wa, wb, wo, norm_w, layer, tm):
    T = x2d.shape[0]
    return pl.pallas_call(
        _mix_kernel,
        grid=(T // tm,),
        in_specs=[
            pl.BlockSpec((tm, D_MODEL), lambda i: (i, 0)),
            pl.BlockSpec((tm, HG_WIDTH), lambda i: (i, 0)),
            pl.BlockSpec((tm, DA_WIDTH), lambda i: (i, 0)),
            pl.BlockSpec((tm, 2 * D_MODEL), lambda i: (i, 0)),
            _const_spec((None, HG_WIDTH, D_MODEL), lambda i: (layer, 0, 0)),
            _const_spec((None, DA_WIDTH, D_MODEL), lambda i: (layer, 0, 0)),
            _const_spec((None, D_MODEL, D_MODEL), lambda i: (layer, 0, 0)),
            _const_spec((None, 1, D_MODEL), lambda i: (layer, 0, 0)),
        ],
        out_specs=pl.BlockSpec((tm, D_MODEL), lambda i: (i, 0)),
        out_shape=jax.ShapeDtypeStruct((T, D_MODEL), F32),
        compiler_params=pltpu.CompilerParams(
            dimension_semantics=("parallel",), vmem_limit_bytes=VMEM_LIMIT),
        name="mix_out",
    )(x2d, a2d, b2d, gate2d, wa, wb, wo, norm_w)


FFN_HALO = 8
FFN_TF = 256


GELU_C = math.sqrt(2.0 / math.pi)
GELU_A = 0.044715


def _ffn_kernel(x_ref, xp_ref, xn_ref, p_ref, nwpre_ref, wup_ref, cw_ref, cb_ref, wdn_ref,
                nwpost_ref, wple_ref, wpg_ref, nwple_ref, out_ref, h_ref, slab_ref, act_ref,
                *, tm, tiles_per_seq):
    i = pl.program_id(0)
    pos = i % tiles_per_seq
    nw = nwpre_ref[...]
    x = x_ref[...]
    n = SUBLANES
    nch = tm // HG_CHUNK
    e = _dot(p_ref[...].astype(BF16), wple_ref[...])
    _to_slabs(slab_ref, _rms(x, nw))
    h_ref[0:tm, :] = _chunk_transposed(slab_ref, tm).astype(BF16)
    hp = _rms(xp_ref[...], nw) * jnp.where(pos == 0, 0.0, 1.0)
    hn = _rms(xn_ref[...], nw) * jnp.where(pos == tiles_per_seq - 1, 0.0, 1.0)
    h_ref[tm:, :] = jnp.concatenate([hp, hn], axis=0).astype(BF16)
    row = lax.broadcasted_iota(jnp.int32, (nch, n, FFN_TF), 1)

    def conv(u, col, scale):
        body = u[0:tm].reshape(nch, n, n, FFN_TF)
        before = u[tm:tm + n]
        after = u[tm + n:tm + 2 * n]
        first, last = body[:, 0], body[:, n - 1]
        prevs = jnp.concatenate([before[None], last[:-1]], axis=0)
        nexts = jnp.concatenate([first[1:], after[None]], axis=0)
        dn0 = jnp.where(row == 0, pltpu.roll(prevs, 1, axis=1), pltpu.roll(last, 1, axis=1))
        up7 = jnp.where(row == n - 1, pltpu.roll(nexts, n - 1, axis=1),
                        pltpu.roll(first, n - 1, axis=1))
        dn = jnp.concatenate([dn0[:, None], body[:, :n - 1]], axis=1)
        up = jnp.concatenate([body[:, 1:], up7[:, None]], axis=1)
        cw = cw_ref[:, col] * scale
        return dn * cw[0:1] + body * cw[1:2] + up * cw[2:3] + cb_ref[:, col] * scale

    for c in range(D_FF // FFN_TF):
        gcol = slice(c * FFN_TF, (c + 1) * FFN_TF)
        vcol = slice(D_FF + c * FFN_TF, D_FF + (c + 1) * FFN_TF)
        h = h_ref[...]
        g = conv(_dot(h, wup_ref[:, gcol]), gcol, 1.0)
        half_val = conv(_dot(h, wup_ref[:, vcol]), vcol, 0.5)
        t = jnp.tanh(g * (GELU_C + (GELU_C * GELU_A) * (g * g)))
        act_ref[:, gcol] = (g * (1.0 + t) * half_val).reshape(tm, FFN_TF).astype(BF16)

    nblk = 2 if nch % 2 == 0 else 1
    rb = tm // nblk
    blocks = [slice(b * rb, (b + 1) * rb) for b in range(nblk)]
    f = [_dot(act_ref[rows, :], wdn_ref[...]) for rows in blocks]
    x2, g = [], []
    for b, rows in enumerate(blocks):
        _to_slabs(slab_ref, f[b], b * rb)
        x2.append(x[rows] + _rms(_chunk_transposed(slab_ref, rb, b * rb), nwpost_ref[...]))
        g.append(_dot(_rms(x2[b]).astype(BF16), wpg_ref[...]))
    for b, rows in enumerate(blocks):
        out_ref[rows, :] = x2[b] + _rms(_sigmoid(g[b]) * e[rows], nwple_ref[...])


def _ffn_ple(x2d, p2d, nw_pre, w_up, conv_w, conv_b, w_down, nw_post,
             w_ple, w_ple_gate, nw_ple, layer, seq, tm):
    T = x2d.shape[0]
    tiles_per_seq = seq // tm
    hb = tm // FFN_HALO
    nhalo = T // FFN_HALO

    def lw(shape):
        return _const_spec((None,) + shape, lambda i: (layer,) + (0,) * len(shape))

    return pl.pallas_call(
        functools.partial(_ffn_kernel, tm=tm, tiles_per_seq=tiles_per_seq),
        grid=(T // tm,),
        in_specs=[
            pl.BlockSpec((tm, D_MODEL), lambda i: (i, 0)),
            pl.BlockSpec((FFN_HALO, D_MODEL), lambda i: (jnp.maximum(i * hb - 1, 0), 0)),
            pl.BlockSpec((FFN_HALO, D_MODEL), lambda i: (jnp.minimum((i + 1) * hb, nhalo - 1), 0)),
            pl.BlockSpec((None, tm, PLE_DIM), lambda i: (layer, i, 0)),
            lw((1, D_MODEL)),
            lw((D_MODEL, 2 * D_FF)),
            lw((3, 2 * D_FF)),
            lw((1, 2 * D_FF)),
            lw((D_FF, D_MODEL)),
            lw((1, D_MODEL)),
            lw((PLE_DIM, D_MODEL)),
            lw((D_MODEL, D_MODEL)),
            lw((1, D_MODEL)),
        ],
        out_specs=pl.BlockSpec((tm, D_MODEL), lambda i: (i, 0)),
        out_shape=jax.ShapeDtypeStruct((T, D_MODEL), F32),
        scratch_shapes=[
            pltpu.VMEM((tm + 2 * FFN_HALO, D_MODEL), BF16),
            pltpu.VMEM((D_MODEL // LANES, tm, LANES), F32),
            pltpu.VMEM((tm, D_FF), BF16),
        ],
        compiler_params=pltpu.CompilerParams(
            dimension_semantics=("parallel",), vmem_limit_bytes=VMEM_LIMIT),
        name="convffn_ple",
    )(x2d, x2d, x2d, p2d, nw_pre, w_up, conv_w, conv_b, w_down, nw_post,
      w_ple, w_ple_gate, nw_ple)


def _rope_tables(L):
    inv = 1.0 / (ROPE_THETA ** (jnp.arange(0, DA_DH, 2, dtype=F32) / DA_DH))
    ang = jnp.arange(L, dtype=F32)[:, None] * inv[None, :]
    cos = jnp.tile(jnp.cos(ang), (1, 4))
    sin = jnp.sin(ang)
    sin_signed = jnp.tile(jnp.concatenate([-sin, sin], axis=-1), (1, 2))
    return cos, sin_signed


def _trunk(x, p, w, tm_in, tm_mix, tm_ffn, tq, hg_heads=2):
    B, L, _ = x.shape
    T = B * L
    cos, sin = _rope_tables(L)
    x2d = x.reshape(T, D_MODEL)
    p3d = p.reshape(DEPTH, T, PLE_DIM)
    for l in range(DEPTH):
        (qh, vh, kf, kb, lgf, lgb, gg, q, k, v, gate) = _in_proj(
            x2d, w["norm_mix_pre"], w["w_in"], cos, sin, w["lb_logits"], l, L, tm_in)
        seq3 = lambda t: t.reshape(B, L, HG_WIDTH)
        a = _hgrn(seq3(qh), seq3(vh), seq3(kf), seq3(kb), seq3(lgf), seq3(lgb), seq3(gg),
                  w["hgrn_gnorm"], l, hg_heads)
        bb = _attention(q.reshape(B, L, DA_WIDTH), k.reshape(B, L, DA_WIDTH),
                        v.reshape(B, L, DA_WIDTH), w["diff_lambda"], w["diff_subln"], l, tq)
        x2d = _mix(x2d, a.reshape(T, HG_WIDTH), bb.reshape(T, DA_WIDTH), gate,
                   w["w_branch_a"], w["w_branch_b"], w["w_out"], w["norm_mix_post"], l, tm_mix)
        x2d = _ffn_ple(x2d, p3d, w["norm_ffn_pre"], w["w_up"], w["conv_w"], w["conv_b"],
                       w["w_down"], w["norm_ffn_post"], w["w_ple"], w["w_ple_gate"],
                       w["norm_ple"], l, L, tm_ffn)
    return x2d.reshape(B, L, D_MODEL)


def _prep_weights(w_in, hgrn_lb_logits, hgrn_gnorm, diff_lambda, diff_subln, w_branch_a,
                  w_branch_b, w_out, norm_mix_pre, norm_mix_post, w_up, conv_w, conv_b, w_down,
                  norm_ffn_pre, norm_ffn_post, w_ple, w_ple_gate, norm_ple):
    def vec(a):
        return a.reshape(DEPTH, 1, a.shape[-1]).astype(F32)

    return {
        "w_in": w_in.astype(BF16),
        "lb_logits": hgrn_lb_logits.reshape(2 * DEPTH, HG_WIDTH).astype(F32),
        "hgrn_gnorm": vec(hgrn_gnorm),
        "diff_lambda": diff_lambda.astype(F32),
        "diff_subln": vec(diff_subln),
        "w_branch_a": w_branch_a.astype(BF16),
        "w_branch_b": w_branch_b.astype(BF16),
        "w_out": w_out.astype(BF16),
        "norm_mix_pre": vec(norm_mix_pre),
        "norm_mix_post": vec(norm_mix_post),
        "w_up": w_up.astype(BF16),
        "conv_w": conv_w.astype(F32),
        "conv_b": vec(conv_b),
        "w_down": w_down.astype(BF16),
        "norm_ffn_pre": vec(norm_ffn_pre),
        "norm_ffn_post": vec(norm_ffn_post),
        "w_ple": w_ple.astype(BF16),
        "w_ple_gate": w_ple_gate.astype(BF16),
        "norm_ple": vec(norm_ple),
    }


def _tile(n, pref):
    t = min(n, pref)
    assert n % t == 0
    return t


def kernel(x_prompt, x_sample, p_prompt, p_sample, w_in, hgrn_lb_logits, hgrn_gnorm, diff_lambda, diff_subln, w_branch_a, w_branch_b, w_out, norm_mix_pre, norm_mix_post, w_up, conv_w, conv_b, w_down, norm_ffn_pre, norm_ffn_post, w_ple, w_ple_gate, norm_ple):
    w = _prep_weights(w_in, hgrn_lb_logits, hgrn_gnorm, diff_lambda, diff_subln, w_branch_a,
                      w_branch_b, w_out, norm_mix_pre, norm_mix_post, w_up, conv_w, conv_b,
                      w_down, norm_ffn_pre, norm_ffn_post, w_ple, w_ple_gate, norm_ple)
    outs = []
    for x, p in ((x_prompt, p_prompt), (x_sample, p_sample)):
        L = x.shape[1]
        hg_heads = HG_HEADS if L <= 2048 else HG_HEADS // 2
        outs.append(_trunk(x, p, w, tm_in=_tile(L, 512), tm_mix=_tile(L, 512),
                           tm_ffn=_tile(L, 512), tq=_tile(L, 512), hg_heads=hg_heads))
    return tuple(outs)
```

```python
import functools
import math

import jax
import jax.numpy as jnp
from jax import lax
from jax.experimental import pallas as pl
from jax.experimental.pallas import tpu as pltpu

D_MODEL = 1024
DEPTH = 4
PLE_DIM = 256
HG_WIDTH = D_MODEL // 2
HG_HEADS = 4
HG_DK = HG_WIDTH // HG_HEADS
LB_FLOOR = 1e-30
DA_WIDTH = D_MODEL // 2
DA_HEADS = 4
DA_DV = DA_WIDTH // DA_HEADS
DA_DH = DA_DV // 2
D_FF = 2816
ROPE_THETA = 10000.0
EPS = 1e-6
IN_COLS = 5 * HG_WIDTH + 3 * DA_WIDTH + 2 * D_MODEL

LANES = 128
SUBLANES = 8
HG_CHUNK = SUBLANES * SUBLANES
HG_CHAINS = 32
VMEM_LIMIT = 56 * 1024 * 1024

BF16 = jnp.bfloat16
F32 = jnp.float32
LOG2E = math.log2(math.e)


def _dot(a, b):
    return jnp.dot(a, b, preferred_element_type=F32)


def _dot_nt(a, b):
    return lax.dot_general(a, b, (((1,), (1,)), ((), ())), preferred_element_type=F32)


def _dot_tn(a, b):
    return lax.dot_general(a, b, (((0,), (0,)), ((), ())), preferred_element_type=F32)


def _rms(x, w=None):
    y = x * lax.rsqrt(jnp.mean(x * x, axis=-1, keepdims=True) + EPS)
    return y if w is None else y * w


def _sigmoid(x):
    return 1.0 / (1.0 + jnp.exp(-x))


def _const_spec(block_shape, index_map):
    return pl.BlockSpec(block_shape, index_map, pipeline_mode=pl.Buffered(1))


def _to_slabs(slab_ref, val, row0=0):
    for s in range(val.shape[1] // LANES):
        slab_ref[s, row0:row0 + val.shape[0], :] = val[:, s * LANES:(s + 1) * LANES]


def _chunk_transposed(slab_ref, rows, row0=0):
    n = SUBLANES
    return jnp.concatenate(
        [jnp.concatenate([slab_ref[s, pl.ds(row0 + c * HG_CHUNK + j, n, stride=n), :]
                          for s in range(slab_ref.shape[0])], axis=1)
         for c in range(rows // HG_CHUNK) for j in range(n)], axis=0)


def _rope(z, cos, sin_signed, first_half):
    rot = jnp.where(first_half, pltpu.roll(z, 96, axis=1), pltpu.roll(z, 32, axis=1))
    return z * cos + rot * sin_signed


def _lower_bounds(lbl_ref, layer):
    out = []
    for d in range(2):
        rows = [lbl_ref[2 * l + d:2 * l + d + 1, :] for l in range(DEPTH)]
        mx = functools.reduce(jnp.maximum, rows)
        ex = [jnp.exp(r - mx) for r in rows]
        den = functools.reduce(lambda a_, b_: a_ + b_, ex)
        sm = [e / den for e in ex]
        cs = sm[0]
        for l in range(1, layer + 1):
            cs = cs + sm[l]
        out.append(cs - sm[0])
    return out


def _hgrn_gates(z, lb):
    t = jnp.exp(-jnp.abs(z))
    r = 1.0 / (1.0 + t)
    tr = t * r
    pos = z > 0
    sig_pos = jnp.where(pos, r, tr)
    sig_neg = jnp.where(pos, tr, r)
    one_m_lb = 1.0 - lb
    f = jnp.maximum(lb, LB_FLOOR) + one_m_lb * sig_pos
    return one_m_lb * sig_neg, jnp.log2(f)


def _in_proj_kernel(x_ref, nw_ref, w_ref, cos_ref, sin_ref, lbl_ref,
                    qh_ref, vh_ref, kf_ref, kb_ref, lgf_ref, lgb_ref, gg_ref,
                    q_ref, k_ref, v_ref, gate_ref, hn_ref, *, layer, tm):
    hn = _rms(x_ref[...], nw_ref[...])
    h = hn.astype(BF16)
    _to_slabs(hn_ref, hn)
    hp = _chunk_transposed(hn_ref, tm).astype(BF16)
    cw = HG_WIDTH
    lb = _lower_bounds(lbl_ref, layer)
    cos = cos_ref[...]
    sin = sin_ref[...]
    lane = lax.broadcasted_iota(jnp.int32, cos.shape, 1)
    first_half = (lane & (DA_DH - 1)) < (DA_DH // 2)
    base = 5 * HG_WIDTH
    gbase = base + 3 * DA_WIDTH

    def col(lhs, c):
        return _dot(lhs, w_ref[:, c * cw:(c + 1) * cw])

    def merge_gate(c):
        gate_ref[:, c * cw:(c + 1) * cw] = col(h, gbase // cw + c).astype(BF16)

    def rope_out(out_ref, z, scale):
        for hd in range(DA_HEADS):
            sl = slice(hd * LANES, (hd + 1) * LANES)
            out_ref[:, sl] = (_rope(z[:, sl], cos, sin, first_half) * scale).astype(BF16)

    zq_d = col(h, base // cw)
    zk_d = col(h, base // cw + 1)
    rope_out(q_ref, zq_d, LOG2E * DA_DH ** -0.5)
    rope_out(k_ref, zk_d, 1.0)
    zq = col(hp, 0)
    qh_ref[...] = (zq * _sigmoid(zq) * (HG_DK ** -0.5)).astype(BF16)
    kf, lgf = _hgrn_gates(col(hp, 1), lb[0])
    kf_ref[...] = kf.astype(BF16)
    lgf_ref[...] = lgf
    kb, lgb = _hgrn_gates(col(hp, 2), lb[1])
    kb_ref[...] = kb.astype(BF16)
    lgb_ref[...] = lgb
    vh_ref[...] = col(hp, 3).astype(BF16)
    zg = col(h, 4)
    gg_ref[...] = (zg * _sigmoid(zg)).astype(BF16)
    v_ref[...] = col(h, base // cw + 2).astype(BF16)
    for c in range(4):
        merge_gate(c)


def _in_proj(x2d, norm_w, w_in, cos, sin, lb_logits2d, layer, seq, tm):
    T = x2d.shape[0]
    npos = seq // tm
    half = lambda dt: (pl.BlockSpec((tm, HG_WIDTH), lambda i: (i, 0)),
                       jax.ShapeDtypeStruct((T, HG_WIDTH), dt))
    outs = [half(BF16), half(BF16), half(BF16), half(BF16), half(F32), half(F32), half(BF16),
            half(BF16), half(BF16), half(BF16),
            (pl.BlockSpec((tm, 2 * D_MODEL), lambda i: (i, 0)),
             jax.ShapeDtypeStruct((T, 2 * D_MODEL), BF16))]
    return pl.pallas_call(
        functools.partial(_in_proj_kernel, layer=layer, tm=tm),
        grid=(T // tm,),
        in_specs=[
            pl.BlockSpec((tm, D_MODEL), lambda i: (i, 0)),
            _const_spec((None, 1, D_MODEL), lambda i: (layer, 0, 0)),
            _const_spec((None, D_MODEL, IN_COLS), lambda i: (layer, 0, 0)),
            pl.BlockSpec((tm, LANES), lambda i: (i % npos, 0)),
            pl.BlockSpec((tm, LANES), lambda i: (i % npos, 0)),
            _const_spec((2 * DEPTH, HG_WIDTH), lambda i: (0, 0)),
        ],
        out_specs=[o[0] for o in outs],
        out_shape=[o[1] for o in outs],
        scratch_shapes=[pltpu.VMEM((D_MODEL // LANES, tm, LANES), F32)],
        compiler_params=pltpu.CompilerParams(
            dimension_semantics=("parallel",), vmem_limit_bytes=VMEM_LIMIT),
        name="in_proj",
    )(x2d, norm_w, w_in, cos, sin, lb_logits2d)


def _hgrn_consts():
    blk = jnp.arange(SUBLANES)[:, None]
    cblk = (jnp.arange(HG_CHUNK) % SUBLANES)[None, :]
    masks = []
    for rev in (False, True):
        for m in (1, 2, 4):
            same = (blk // (2 * m)) == (cblk // (2 * m))
            r_hi = (blk % (2 * m)) >= m
            c_hi = (cblk % (2 * m)) >= m
            mk = same & (r_hi & ~c_hi if not rev else ~r_hi & c_hi)
            masks.append(mk)
    masks.append(blk == cblk)
    masks = jnp.stack(masks).astype(F32)
    grp = (jnp.arange(SUBLANES * LANES) // LANES)[:, None]
    cgrp = (jnp.arange(HG_CHUNK) // SUBLANES)[None, :]
    wsum = (grp == cgrp).astype(BF16)
    return masks, wsum


def _gather_rows(x, src, row):
    out = None
    for s in sorted({(i - src[i]) % SUBLANES for i in range(SUBLANES)}):
        rolled = x if s == 0 else pltpu.roll(x, s, axis=0)
        if out is None:
            out = rolled
        else:
            sel = functools.reduce(
                jnp.logical_or,
                [row == i for i in range(SUBLANES) if (i - src[i]) % SUBLANES == s])
            out = jnp.where(sel, rolled, out)
    return out


def _hgrn_prep(qb, kb, vb, lgc, rev, row):
    n = SUBLANES
    split = lambda a_: [a_[j * n:(j + 1) * n, :] for j in range(n)]
    q = split(qb.astype(F32))
    kk = split(kb.astype(F32))
    lg = split(lgc)
    order = list(range(n - 1, -1, -1)) if rev else list(range(n))
    w = [None] * n
    acc = None
    for j in order:
        acc = lg[j] if acc is None else acc + lg[j]
        w[j] = acc
    tot = acc
    incl = tot
    for s in (1, 2, 4):
        if not rev:
            incl = incl + jnp.where(row >= s, pltpu.roll(incl, s, axis=0), 0.0)
        else:
            incl = incl + jnp.where(row < n - s, pltpu.roll(incl, n - s, axis=0), 0.0)
    if not rev:
        excl = jnp.where(row >= 1, pltpu.roll(incl, 1, axis=0), 0.0)
        chunk_tot = jnp.broadcast_to(incl[n - 1:n, :], incl.shape)
    else:
        excl = jnp.where(row < n - 1, pltpu.roll(incl, n - 1, axis=0), 0.0)
        chunk_tot = jnp.broadcast_to(incl[0:1, :], incl.shape)
    b = [w[j] + excl for j in range(n)]

    bcat = jnp.concatenate(b, axis=0)

    c = {"vb": vb, "rev": rev}
    c["qe"] = qb * jnp.exp2(bcat).astype(BF16)

    c["lv"] = []
    for m in (1, 2, 4):
        if not rev:
            src = [(i // (2 * m)) * 2 * m + m - 1 for i in range(n)]
        else:
            src = [(i // (2 * m)) * 2 * m + m for i in range(n)]
        y = excl - _gather_rows(incl, src, row)
        e = jnp.concatenate([jnp.exp2(-jnp.abs(w[j] + y)) for j in range(n)], axis=0).astype(BF16)
        c["lv"].append((qb * e, kb * e))

    zero = jnp.zeros_like(q[0])
    rows_p = []
    for j in range(n):
        blocks = []
        for jp in range(n):
            earlier = (jp <= j) if not rev else (jp >= j)
            if not earlier:
                blocks.append(zero)
            elif jp == j:
                blocks.append(q[j] * kk[j])
            else:
                blocks.append(q[j] * kk[jp] * jnp.exp2(w[j] - w[jp]))
        rows_p.append(jnp.concatenate(blocks, axis=1))
    c["pbig"] = jnp.concatenate(rows_p, axis=0).astype(BF16)

    tot_cat = jnp.concatenate([chunk_tot] * n, axis=0)
    c["kd"] = kb * jnp.exp2(tot_cat - bcat).astype(BF16)
    c["dec"] = jnp.exp2(chunk_tot[0:1, :])
    return c


def _hgrn_mm1(c, st_ref, wsum_ref):
    st = st_ref[...]
    c["st"] = st
    c["o"] = _dot_nt(c["qe"], st.astype(BF16))
    c["am"] = [_dot_nt(qm, km) for qm, km in c["lv"]]
    c["ad"] = _dot(c["pbig"], wsum_ref[...])
    c["kv"] = _dot_tn(c["vb"], c["kd"])


def _hgrn_mid(c, sel):
    n = SUBLANES
    pieces = []
    for j in range(n):
        rows = slice(j * n, (j + 1) * n)
        a = jnp.where(sel[6], c["ad"][rows], 0.0)
        for li, am in enumerate(c["am"]):
            a = jnp.where(sel[(3 if c["rev"] else 0) + li], am[rows], a)
        pieces.append(a)
    c["a"] = jnp.concatenate(pieces, axis=0).astype(BF16)


def _hgrn_mm2(c, st_ref):
    st_ref[...] = c["dec"] * c["st"] + c["kv"]
    return c["o"] + _dot(c["a"], c["vb"])


def _hgrn_kernel(qh_ref, vh_ref, kf_ref, kb_ref, lgf_ref, lgb_ref, gg_ref, gn_ref, masks_ref,
                 wsum_ref, out_ref, o_ref, st_ref, *, seq, heads):
    n = SUBLANES
    nchunks = seq // HG_CHUNK
    unroll = max(1, HG_CHAINS // (2 * heads))
    while (nchunks // 2) % unroll:
        unroll //= 2
    row = lax.broadcasted_iota(jnp.int32, (n, HG_DK), 0)
    sel = [masks_ref[i] > 0.5 for i in range(masks_ref.shape[0])]
    st_ref[...] = jnp.zeros_like(st_ref)

    def make_body(accumulate):
        def body(it, carry):
            chains = [(hd, d) for hd in range(heads) for d in range(2)]
            ctx, bases = [], []
            for u in range(unroll):
                c = it * unroll + u
                fw = pl.multiple_of(c * HG_CHUNK, HG_CHUNK)
                bw = pl.multiple_of((nchunks - 1 - c) * HG_CHUNK, HG_CHUNK)
                bases.append((fw, bw))
                step = []
                for hd, d in chains:
                    rows = pl.ds(bw if d else fw, HG_CHUNK)
                    cols = slice(hd * HG_DK, (hd + 1) * HG_DK)
                    k_ref, lg_ref = (kb_ref, lgb_ref) if d else (kf_ref, lgf_ref)
                    step.append(_hgrn_prep(qh_ref[rows, cols], k_ref[rows, cols],
                                           vh_ref[rows, cols], lg_ref[rows, cols], d == 1, row))
                ctx.append(step)
            for u in range(unroll):
                for (hd, d), cx in zip(chains, ctx[u]):
                    _hgrn_mm1(cx, st_ref.at[2 * hd + d], wsum_ref)
                for cx in ctx[u]:
                    _hgrn_mid(cx, sel)
                for (hd, d), cx in zip(chains, ctx[u]):
                    o = _hgrn_mm2(cx, st_ref.at[2 * hd + d])
                    base = bases[u][d]
                    for j in range(n):
                        dst = (hd, pl.ds(base + j, n, stride=n), slice(None))
                        piece = o[j * n:(j + 1) * n, :]
                        o_ref[dst] = o_ref[dst] + piece if accumulate else piece
            return carry
        return body

    half_iters = nchunks // (2 * unroll)
    lax.fori_loop(0, half_iters, make_body(False), 0)
    lax.fori_loop(half_iters, 2 * half_iters, make_body(True), 0)

    gn = gn_ref[...]
    rows_e = min(256, seq)

    def epilogue(c, carry):
        sl = pl.ds(pl.multiple_of(c * rows_e, rows_e), rows_e)
        for hd in range(heads):
            cols = slice(hd * HG_DK, (hd + 1) * HG_DK)
            y = _rms(o_ref[hd, sl, :], gn) * gg_ref[sl, cols].astype(F32)
            out_ref[sl, cols] = y.astype(out_ref.dtype)
        return carry

    lax.fori_loop(0, seq // rows_e, epilogue, 0)


def _hgrn(qh, vh, kf, kb, lgf, lgb, gg, gnorm, layer, heads):
    B, L, _ = qh.shape
    masks, wsum = _hgrn_consts()
    width = heads * HG_DK
    blk = pl.BlockSpec((None, L, width), lambda b, h: (b, 0, h))
    return pl.pallas_call(
        functools.partial(_hgrn_kernel, seq=L, heads=heads),
        grid=(B, HG_HEADS // heads),
        in_specs=[
            blk, blk, blk, blk, blk, blk, blk,
            _const_spec((None, 1, HG_DK), lambda b, h: (layer, 0, 0)),
            _const_spec(masks.shape, lambda b, h: (0, 0, 0)),
            _const_spec(wsum.shape, lambda b, h: (0, 0)),
        ],
        out_specs=blk,
        out_shape=jax.ShapeDtypeStruct((B, L, HG_WIDTH), BF16),
        scratch_shapes=[
            pltpu.VMEM((heads, L, HG_DK), F32),
            pltpu.VMEM((2 * heads, HG_DK, HG_DK), F32),
        ],
        compiler_params=pltpu.CompilerParams(
            dimension_semantics=("parallel", "parallel"), vmem_limit_bytes=VMEM_LIMIT),
        name="hgrn2",
    )(qh, vh, kf, kb, lgf, lgb, gg, gnorm, masks, wsum)


ATTN_SUB = 128


ATTN_SLACK = 1.0 + 2.0 ** -6
ATTN_MIN_SUM = 2.0 ** -64


def _attn_combine(p, l, lam, v, sw, lambda_init):
    sub, seq = p[0].shape
    w = [jnp.broadcast_to(sc, (sub, LANES)).astype(BF16) for sc in (1.0 / l[0], lam / l[1])]
    a = jnp.concatenate(
        [p[0][:, t * LANES:(t + 1) * LANES] * w[0] - p[1][:, t * LANES:(t + 1) * LANES] * w[1]
         for t in range(seq // LANES)], axis=1)
    return _rms(_dot(a, v), sw) * (1.0 - lambda_init)


def _attn_kernel(q_ref, k_ref, v_ref, lam_ref, sw_ref, out_ref, kaug_ref, kmax_ref,
                 *, lambda_init):
    lp = lam_ref[...]
    lam = (jnp.exp(jnp.sum(lp[0:1] * lp[1:2], axis=-1, keepdims=True))
           - jnp.exp(jnp.sum(lp[2:3] * lp[3:4], axis=-1, keepdims=True)) + lambda_init)
    sw = sw_ref[...]
    tq = q_ref.shape[0]
    seq = k_ref.shape[0]
    sub = min(ATTN_SUB, tq)
    nsub = tq // sub
    spare_lane = (DA_DH, 0)

    def own_lanes(lane, i):
        return (lane < DA_DH) if i == 0 else (lane >= DA_DH)

    @pl.when(pl.program_id(2) == 0)
    def _():
        k = k_ref[...]
        k32 = k.astype(F32)
        sq = k32 * k32
        lane = lax.broadcasted_iota(jnp.int32, k.shape, 1)
        for i in range(2):
            own = own_lanes(lane, i)
            n2 = jnp.sum(jnp.where(own, sq, 0.0), axis=-1, keepdims=True)
            kmax_ref[i] = jnp.broadcast_to(jnp.sqrt(jnp.max(n2, axis=0, keepdims=True)),
                                           kmax_ref.shape[1:])
            kaug_ref[i] = jnp.where(
                own, k32, jnp.where(lane == spare_lane[i], -1.0, 0.0)).astype(BF16)

    qa = []
    for r in range(nsub):
        q32 = q_ref[r * sub:(r + 1) * sub, :].astype(F32)
        sq = q32 * q32
        lane = lax.broadcasted_iota(jnp.int32, q32.shape, 1)
        pair = []
        for i in range(2):
            own = own_lanes(lane, i)
            qn = jnp.sqrt(jnp.sum(jnp.where(own, sq, 0.0), axis=-1, keepdims=True))
            m = qn * kmax_ref[i][0:1, 0:1] * ATTN_SLACK
            pair.append(
                jnp.where(own, q32, jnp.where(lane == spare_lane[i], m, 0.0)).astype(BF16))
        qa.append(pair)
    scores = [[_dot_nt(qa[r][i], kaug_ref[i]) for i in range(2)] for r in range(nsub)]
    lmin = None
    for r in range(nsub):
        p, l = [], []
        for i in range(2):
            e = jnp.exp2(scores[r][i])
            l.append(jnp.sum(e, axis=-1, keepdims=True))
            p.append(e.astype(BF16))
            lmin = l[i] if lmin is None else jnp.minimum(lmin, l[i])
        out_ref[r * sub:(r + 1) * sub, :] = _attn_combine(
            p, l, lam, v_ref[...], sw, lambda_init).astype(out_ref.dtype)
    shifted_ok = jnp.min(lmin) >= ATTN_MIN_SUM

    @pl.when(jnp.logical_not(shifted_ok))
    def _():
        k = k_ref[...]
        for r in range(nsub):
            q = q_ref[r * sub:(r + 1) * sub, :]
            lane = lax.broadcasted_iota(jnp.int32, q.shape, 1)
            zero = jnp.zeros_like(q)
            p, l = [], []
            for i in range(2):
                s = _dot_nt(jnp.where(own_lanes(lane, i), q, zero), k)
                e = jnp.exp2(s - jnp.max(s, axis=-1, keepdims=True))
                l.append(jnp.sum(e, axis=-1, keepdims=True))
                p.append(e.astype(BF16))
            out_ref[r * sub:(r + 1) * sub, :] = _attn_combine(
                p, l, lam, v_ref[...], sw, lambda_init).astype(out_ref.dtype)


def _attention(q3d, k3d, v3d, lam_params, subln, layer, tq):
    B, L, _ = q3d.shape
    lambda_init = 0.8 - 0.6 * math.exp(-0.3 * layer)
    return pl.pallas_call(
        functools.partial(_attn_kernel, lambda_init=lambda_init),
        grid=(B, DA_HEADS, L // tq),
        in_specs=[
            pl.BlockSpec((None, tq, DA_DV), lambda b, h, i: (b, i, h)),
            pl.BlockSpec((None, L, DA_DV), lambda b, h, i: (b, 0, h)),
            pl.BlockSpec((None, L, DA_DV), lambda b, h, i: (b, 0, h)),
            _const_spec((None, 4, DA_DH), lambda b, h, i: (layer, 0, 0)),
            _const_spec((None, 1, DA_DV), lambda b, h, i: (layer, 0, 0)),
        ],
        out_specs=pl.BlockSpec((None, tq, DA_DV), lambda b, h, i: (b, i, h)),
        out_shape=jax.ShapeDtypeStruct((B, L, DA_WIDTH), BF16),
        scratch_shapes=[
            pltpu.VMEM((2, L, DA_DV), BF16),
            pltpu.VMEM((2, SUBLANES, LANES), F32),
        ],
        compiler_params=pltpu.CompilerParams(
            dimension_semantics=("parallel", "parallel", "arbitrary"),
            vmem_limit_bytes=VMEM_LIMIT),
        name="diff_attn",
    )(q3d, k3d, v3d, lam_params, subln)


def _mix_kernel(x_ref, a_ref, b_ref, gate_ref, wa_ref, wb_ref, wo_ref, nw_ref, out_ref):
    ga = _sigmoid(gate_ref[:, :D_MODEL].astype(F32))
    gb = _sigmoid(gate_ref[:, D_MODEL:].astype(F32))
    m = ga * _dot(a_ref[...], wa_ref[...]) + gb * _dot(b_ref[...], wb_ref[...])
    r = _dot(m.astype(BF16), wo_ref[...])
    out_ref[...] = x_ref[...] + _rms(r, nw_ref[...])


def _mix(x2d, a2d, b2d, gate2d, wa, wb, wo, norm_w, layer, tm):
    T = x2d.shape[0]
    return pl.pallas_call(
        _mix_kernel,
        grid=(T // tm,),
        in_specs=[
            pl.BlockSpec((tm, D_MODEL), lambda i: (i, 0)),
            pl.BlockSpec((tm, HG_WIDTH), lambda i: (i, 0)),
            pl.BlockSpec((tm, DA_WIDTH), lambda i: (i, 0)),
            pl.BlockSpec((tm, 2 * D_MODEL), lambda i: (i, 0)),
            _const_spec((None, HG_WIDTH, D_MODEL), lambda i: (layer, 0, 0)),
            _const_spec((None, DA_WIDTH, D_MODEL), lambda i: (layer, 0, 0)),
            _const_spec((None, D_MODEL, D_MODEL), lambda i: (layer, 0, 0)),
            _const_spec((None, 1, D_MODEL), lambda i: (layer, 0, 0)),
        ],
        out_specs=pl.BlockSpec((tm, D_MODEL), lambda i: (i, 0)),
        out_shape=jax.ShapeDtypeStruct((T, D_MODEL), F32),
        compiler_params=pltpu.CompilerParams(
            dimension_semantics=("parallel",), vmem_limit_bytes=VMEM_LIMIT),
        name="mix_out",
    )(x2d, a2d, b2d, gate2d, wa, wb, wo, norm_w)


FFN_HALO = 8
FFN_TF = 256


GELU_C = math.sqrt(2.0 / math.pi)
GELU_A = 0.044715


def _ffn_kernel(x_ref, xp_ref, xn_ref, p_ref, nwpre_ref, wup_ref, cw_ref, cb_ref, wdn_ref,
                nwpost_ref, wple_ref, wpg_ref, nwple_ref, out_ref, h_ref, slab_ref, act_ref,
                *, tm, tiles_per_seq):
    i = pl.program_id(0)
    pos = i % tiles_per_seq
    nw = nwpre_ref[...]
    x = x_ref[...]
    n = SUBLANES
    nch = tm // HG_CHUNK
    e = _dot(p_ref[...].astype(BF16), wple_ref[...])
    _to_slabs(slab_ref, _rms(x, nw))
    h_ref[0:tm, :] = _chunk_transposed(slab_ref, tm).astype(BF16)
    hp = _rms(xp_ref[...], nw) * jnp.where(pos == 0, 0.0, 1.0)
    hn = _rms(xn_ref[...], nw) * jnp.where(pos == tiles_per_seq - 1, 0.0, 1.0)
    h_ref[tm:, :] = jnp.concatenate([hp, hn], axis=0).astype(BF16)
    row = lax.broadcasted_iota(jnp.int32, (nch, n, FFN_TF), 1)

    def conv(u, col, scale):
        body = u[0:tm].reshape(nch, n, n, FFN_TF)
        before = u[tm:tm + n]
        after = u[tm + n:tm + 2 * n]
        first, last = body[:, 0], body[:, n - 1]
        prevs = jnp.concatenate([before[None], last[:-1]], axis=0)
        nexts = jnp.concatenate([first[1:], after[None]], axis=0)
        dn0 = jnp.where(row == 0, pltpu.roll(prevs, 1, axis=1), pltpu.roll(last, 1, axis=1))
        up7 = jnp.where(row == n - 1, pltpu.roll(nexts, n - 1, axis=1),
                        pltpu.roll(first, n - 1, axis=1))
        dn = jnp.concatenate([dn0[:, None], body[:, :n - 1]], axis=1)
        up = jnp.concatenate([body[:, 1:], up7[:, None]], axis=1)
        cw = cw_ref[:, col] * scale
        return dn * cw[0:1] + body * cw[1:2] + up * cw[2:3] + cb_ref[:, col] * scale

    for c in range(D_FF // FFN_TF):
        gcol = slice(c * FFN_TF, (c + 1) * FFN_TF)
        vcol = slice(D_FF + c * FFN_TF, D_FF + (c + 1) * FFN_TF)
        h = h_ref[...]
        g = conv(_dot(h, wup_ref[:, gcol]), gcol, 1.0)
        half_val = conv(_dot(h, wup_ref[:, vcol]), vcol, 0.5)
        t = jnp.tanh(g * (GELU_C + (GELU_C * GELU_A) * (g * g)))
        act_ref[:, gcol] = (g * (1.0 + t) * half_val).reshape(tm, FFN_TF).astype(BF16)

    nblk = 2 if nch % 2 == 0 else 1
    rb = tm // nblk
    blocks = [slice(b * rb, (b + 1) * rb) for b in range(nblk)]
    f = [_dot(act_ref[rows, :], wdn_ref[...]) for rows in blocks]
    x2, g = [], []
    for b, rows in enumerate(blocks):
        _to_slabs(slab_ref, f[b], b * rb)
        x2.append(x[rows] + _rms(_chunk_transposed(slab_ref, rb, b * rb), nwpost_ref[...]))
        g.append(_dot(_rms(x2[b]).astype(BF16), wpg_ref[...]))
    for b, rows in enumerate(blocks):
        out_ref[rows, :] = x2[b] + _rms(_sigmoid(g[b]) * e[rows], nwple_ref[...])


def _ffn_ple(x2d, p2d, nw_pre, w_up, conv_w, conv_b, w_down, nw_post,
             w_ple, w_ple_gate, nw_ple, layer, seq, tm):
    T = x2d.shape[0]
    tiles_per_seq = seq // tm
    hb = tm // FFN_HALO
    nhalo = T // FFN_HALO

    def lw(shape):
        return _const_spec((None,) + shape, lambda i: (layer,) + (0,) * len(shape))

    return pl.pallas_call(
        functools.partial(_ffn_kernel, tm=tm, tiles_per_seq=tiles_per_seq),
        grid=(T // tm,),
        in_specs=[
            pl.BlockSpec((tm, D_MODEL), lambda i: (i, 0)),
            pl.BlockSpec((FFN_HALO, D_MODEL), lambda i: (jnp.maximum(i * hb - 1, 0), 0)),
            pl.BlockSpec((FFN_HALO, D_MODEL), lambda i: (jnp.minimum((i + 1) * hb, nhalo - 1), 0)),
            pl.BlockSpec((None, tm, PLE_DIM), lambda i: (layer, i, 0)),
            lw((1, D_MODEL)),
            lw((D_MODEL, 2 * D_FF)),
            lw((3, 2 * D_FF)),
            lw((1, 2 * D_FF)),
            lw((D_FF, D_MODEL)),
            lw((1, D_MODEL)),
            lw((PLE_DIM, D_MODEL)),
            lw((D_MODEL, D_MODEL)),
            lw((1, D_MODEL)),
        ],
        out_specs=pl.BlockSpec((tm, D_MODEL), lambda i: (i, 0)),
        out_shape=jax.ShapeDtypeStruct((T, D_MODEL), F32),
        scratch_shapes=[
            pltpu.VMEM((tm + 2 * FFN_HALO, D_MODEL), BF16),
            pltpu.VMEM((D_MODEL // LANES, tm, LANES), F32),
            pltpu.VMEM((tm, D_FF), BF16),
        ],
        compiler_params=pltpu.CompilerParams(
            dimension_semantics=("parallel",), vmem_limit_bytes=VMEM_LIMIT),
        name="convffn_ple",
    )(x2d, x2d, x2d, p2d, nw_pre, w_up, conv_w, conv_b, w_down, nw_post,
      w_ple, w_ple_gate, nw_ple)


def _rope_tables(L):
    inv = 1.0 / (ROPE_THETA ** (jnp.arange(0, DA_DH, 2, dtype=F32) / DA_DH))
    ang = jnp.arange(L, dtype=F32)[:, None] * inv[None, :]
    cos = jnp.tile(jnp.cos(ang), (1, 4))
    sin = jnp.sin(ang)
    sin_signed = jnp.tile(jnp.concatenate([-sin, sin], axis=-1), (1, 2))
    return cos, sin_signed


def _trunk(x, p, w, tm_in, tm_mix, tm_ffn, tq, hg_heads=2):
    B, L, _ = x.shape
    T = B * L
    cos, sin = _rope_tables(L)
    x2d = x.reshape(T, D_MODEL)
    p3d = p.reshape(DEPTH, T, PLE_DIM)
    for l in range(DEPTH):
        (qh, vh, kf, kb, lgf, lgb, gg, q, k, v, gate) = _in_proj(
            x2d, w["norm_mix_pre"], w["w_in"], cos, sin, w["lb_logits"], l, L, tm_in)
        seq3 = lambda t: t.reshape(B, L, HG_WIDTH)
        a = _hgrn(seq3(qh), seq3(vh), seq3(kf), seq3(kb), seq3(lgf), seq3(lgb), seq3(gg),
                  w["hgrn_gnorm"], l, hg_heads)
        bb = _attention(q.reshape(B, L, DA_WIDTH), k.reshape(B, L, DA_WIDTH),
                        v.reshape(B, L, DA_WIDTH), w["diff_lambda"], w["diff_subln"], l, tq)
        x2d = _mix(x2d, a.reshape(T, HG_WIDTH), bb.reshape(T, DA_WIDTH), gate,
                   w["w_branch_a"], w["w_branch_b"], w["w_out"], w["norm_mix_post"], l, tm_mix)
        x2d = _ffn_ple(x2d, p3d, w["norm_ffn_pre"], w["w_up"], w["conv_w"], w["conv_b"],
                       w["w_down"], w["norm_ffn_post"], w["w_ple"], w["w_ple_gate"],
                       w["norm_ple"], l, L, tm_ffn)
    return x2d.reshape(B, L, D_MODEL)


def _prep_weights(w_in, hgrn_lb_logits, hgrn_gnorm, diff_lambda, diff_subln, w_branch_a,
                  w_branch_b, w_out, norm_mix_pre, norm_mix_post, w_up, conv_w, conv_b, w_down,
                  norm_ffn_pre, norm_ffn_post, w_ple, w_ple_gate, norm_ple):
    def vec(a):
        return a.reshape(DEPTH, 1, a.shape[-1]).astype(F32)

    return {
        "w_in": w_in.astype(BF16),
        "lb_logits": hgrn_lb_logits.reshape(2 * DEPTH, HG_WIDTH).astype(F32),
        "hgrn_gnorm": vec(hgrn_gnorm),
        "diff_lambda": diff_lambda.astype(F32),
        "diff_subln": vec(diff_subln),
        "w_branch_a": w_branch_a.astype(BF16),
        "w_branch_b": w_branch_b.astype(BF16),
        "w_out": w_out.astype(BF16),
        "norm_mix_pre": vec(norm_mix_pre),
        "norm_mix_post": vec(norm_mix_post),
        "w_up": w_up.astype(BF16),
        "conv_w": conv_w.astype(F32),
        "conv_b": vec(conv_b),
        "w_down": w_down.astype(BF16),
        "norm_ffn_pre": vec(norm_ffn_pre),
        "norm_ffn_post": vec(norm_ffn_post),
        "w_ple": w_ple.astype(BF16),
        "w_ple_gate": w_ple_gate.astype(BF16),
        "norm_ple": vec(norm_ple),
    }


def _tile(n, pref):
    t = min(n, pref)
    assert n % t == 0
    return t


def kernel(x_prompt, x_sample, p_prompt, p_sample, w_in, hgrn_lb_logits, hgrn_gnorm, diff_lambda, diff_subln, w_branch_a, w_branch_b, w_out, norm_mix_pre, norm_mix_post, w_up, conv_w, conv_b, w_down, norm_ffn_pre, norm_ffn_post, w_ple, w_ple_gate, norm_ple):
    w = _prep_weights(w_in, hgrn_lb_logits, hgrn_gnorm, diff_lambda, diff_subln, w_branch_a,
                      w_branch_b, w_out, norm_mix_pre, norm_mix_post, w_up, conv_w, conv_b,
                      w_down, norm_ffn_pre, norm_ffn_post, w_ple, w_ple_gate, norm_ple)
    outs = []
    for x, p in ((x_prompt, p_prompt), (x_sample, p_sample)):
        L = x.shape[1]
        hg_heads = HG_HEADS if L <= 2048 else HG_HEADS // 2
        outs.append(_trunk(x, p, w, tm_in=_tile(L, 512), tm_mix=_tile(L, 1024),
                           tm_ffn=_tile(L, 512), tq=_tile(L, 512), hg_heads=hg_heads))
    return tuple(outs)
```

```python
import functools
import math

import jax
import jax.numpy as jnp
from jax import lax
from jax.experimental import pallas as pl
from jax.experimental.pallas import tpu as pltpu

D_MODEL = 1024
DEPTH = 4
PLE_DIM = 256
HG_WIDTH = D_MODEL // 2
HG_HEADS = 4
HG_DK = HG_WIDTH // HG_HEADS
LB_FLOOR = 1e-30
DA_WIDTH = D_MODEL // 2
DA_HEADS = 4
DA_DV = DA_WIDTH // DA_HEADS
DA_DH = DA_DV // 2
D_FF = 2816
ROPE_THETA = 10000.0
EPS = 1e-6
IN_COLS = 5 * HG_WIDTH + 3 * DA_WIDTH + 2 * D_MODEL

LANES = 128
SUBLANES = 8
HG_CHUNK = SUBLANES * SUBLANES
HG_CHAINS = 32
VMEM_LIMIT = 56 * 1024 * 1024

BF16 = jnp.bfloat16
F32 = jnp.float32
LOG2E = math.log2(math.e)


def _dot(a, b):
    return jnp.dot(a, b, preferred_element_type=F32)


def _dot_nt(a, b):
    return lax.dot_general(a, b, (((1,), (1,)), ((), ())), preferred_element_type=F32)


def _dot_tn(a, b):
    return lax.dot_general(a, b, (((0,), (0,)), ((), ())), preferred_element_type=F32)


def _rms(x, w=None):
    y = x * lax.rsqrt(jnp.mean(x * x, axis=-1, keepdims=True) + EPS)
    return y if w is None else y * w


def _sigmoid(x):
    return 1.0 / (1.0 + jnp.exp(-x))


def _const_spec(block_shape, index_map):
    return pl.BlockSpec(block_shape, index_map, pipeline_mode=pl.Buffered(1))


def _to_slabs(slab_ref, val, row0=0):
    for s in range(val.shape[1] // LANES):
        slab_ref[s, row0:row0 + val.shape[0], :] = val[:, s * LANES:(s + 1) * LANES]


def _chunk_transposed(slab_ref, rows, row0=0):
    n = SUBLANES
    return jnp.concatenate(
        [jnp.concatenate([slab_ref[s, pl.ds(row0 + c * HG_CHUNK + j, n, stride=n), :]
                          for s in range(slab_ref.shape[0])], axis=1)
         for c in range(rows // HG_CHUNK) for j in range(n)], axis=0)


def _rope(z, cos, sin_signed, first_half):
    rot = jnp.where(first_half, pltpu.roll(z, 96, axis=1), pltpu.roll(z, 32, axis=1))
    return z * cos + rot * sin_signed


def _lower_bounds(lbl_ref, layer):
    out = []
    for d in range(2):
        rows = [lbl_ref[2 * l + d:2 * l + d + 1, :] for l in range(DEPTH)]
        mx = functools.reduce(jnp.maximum, rows)
        ex = [jnp.exp(r - mx) for r in rows]
        den = functools.reduce(lambda a_, b_: a_ + b_, ex)
        sm = [e / den for e in ex]
        cs = sm[0]
        for l in range(1, layer + 1):
            cs = cs + sm[l]
        out.append(cs - sm[0])
    return out


def _hgrn_gates(z, lb):
    t = jnp.exp(-jnp.abs(z))
    r = 1.0 / (1.0 + t)
    tr = t * r
    pos = z > 0
    sig_pos = jnp.where(pos, r, tr)
    sig_neg = jnp.where(pos, tr, r)
    one_m_lb = 1.0 - lb
    f = jnp.maximum(lb, LB_FLOOR) + one_m_lb * sig_pos
    return one_m_lb * sig_neg, jnp.log2(f)


def _in_proj_kernel(x_ref, nw_ref, w_ref, cos_ref, sin_ref, lbl_ref,
                    qh_ref, vh_ref, kf_ref, kb_ref, lgf_ref, lgb_ref, gg_ref,
                    q_ref, k_ref, v_ref, gate_ref, hn_ref, *, layer, tm):
    hn = _rms(x_ref[...], nw_ref[...])
    h = hn.astype(BF16)
    _to_slabs(hn_ref, hn)
    hp = _chunk_transposed(hn_ref, tm).astype(BF16)
    cw = HG_WIDTH
    lb = _lower_bounds(lbl_ref, layer)
    cos = cos_ref[...]
    sin = sin_ref[...]
    lane = lax.broadcasted_iota(jnp.int32, cos.shape, 1)
    first_half = (lane & (DA_DH - 1)) < (DA_DH // 2)
    base = 5 * HG_WIDTH
    gbase = base + 3 * DA_WIDTH

    def col(lhs, c):
        return _dot(lhs, w_ref[:, c * cw:(c + 1) * cw])

    def merge_gate(c):
        gate_ref[:, c * cw:(c + 1) * cw] = col(h, gbase // cw + c).astype(BF16)

    def rope_out(out_ref, z, scale):
        for hd in range(DA_HEADS):
            sl = slice(hd * LANES, (hd + 1) * LANES)
            out_ref[:, sl] = (_rope(z[:, sl], cos, sin, first_half) * scale).astype(BF16)

    zq_d = col(h, base // cw)
    zk_d = col(h, base // cw + 1)
    rope_out(q_ref, zq_d, LOG2E * DA_DH ** -0.5)
    rope_out(k_ref, zk_d, 1.0)
    zq = col(hp, 0)
    qh_ref[...] = (zq * _sigmoid(zq) * (HG_DK ** -0.5)).astype(BF16)
    kf, lgf = _hgrn_gates(col(hp, 1), lb[0])
    kf_ref[...] = kf.astype(BF16)
    lgf_ref[...] = lgf
    kb, lgb = _hgrn_gates(col(hp, 2), lb[1])
    kb_ref[...] = kb.astype(BF16)
    lgb_ref[...] = lgb
    vh_ref[...] = col(hp, 3).astype(BF16)
    zg = col(h, 4)
    gg_ref[...] = (zg * _sigmoid(zg)).astype(BF16)
    v_ref[...] = col(h, base // cw + 2).astype(BF16)
    for c in range(4):
        merge_gate(c)


def _in_proj(x2d, norm_w, w_in, cos, sin, lb_logits2d, layer, seq, tm):
    T = x2d.shape[0]
    npos = seq // tm
    half = lambda dt: (pl.BlockSpec((tm, HG_WIDTH), lambda i: (i, 0)),
                       jax.ShapeDtypeStruct((T, HG_WIDTH), dt))
    outs = [half(BF16), half(BF16), half(BF16), half(BF16), half(F32), half(F32), half(BF16),
            half(BF16), half(BF16), half(BF16),
            (pl.BlockSpec((tm, 2 * D_MODEL), lambda i: (i, 0)),
             jax.ShapeDtypeStruct((T, 2 * D_MODEL), BF16))]
    return pl.pallas_call(
        functools.partial(_in_proj_kernel, layer=layer, tm=tm),
        grid=(T // tm,),
        in_specs=[
            pl.BlockSpec((tm, D_MODEL), lambda i: (i, 0)),
            _const_spec((None, 1, D_MODEL), lambda i: (layer, 0, 0)),
            _const_spec((None, D_MODEL, IN_COLS), lambda i: (layer, 0, 0)),
            pl.BlockSpec((tm, LANES), lambda i: (i % npos, 0)),
            pl.BlockSpec((tm, LANES), lambda i: (i % npos, 0)),
            _const_spec((2 * DEPTH, HG_WIDTH), lambda i: (0, 0)),
        ],
        out_specs=[o[0] for o in outs],
        out_shape=[o[1] for o in outs],
        scratch_shapes=[pltpu.VMEM((D_MODEL // LANES, tm, LANES), F32)],
        compiler_params=pltpu.CompilerParams(
            dimension_semantics=("parallel",), vmem_limit_bytes=VMEM_LIMIT),
        name="in_proj",
    )(x2d, norm_w, w_in, cos, sin, lb_logits2d)


def _hgrn_consts():
    blk = jnp.arange(SUBLANES)[:, None]
    cblk = (jnp.arange(HG_CHUNK) % SUBLANES)[None, :]
    masks = []
    for rev in (False, True):
        for m in (1, 2, 4):
            same = (blk // (2 * m)) == (cblk // (2 * m))
            r_hi = (blk % (2 * m)) >= m
            c_hi = (cblk % (2 * m)) >= m
            mk = same & (r_hi & ~c_hi if not rev else ~r_hi & c_hi)
            masks.append(mk)
    return jnp.stack(masks).astype(F32)


def _gather_rows(x, src, row):
    out = None
    for s in sorted({(i - src[i]) % SUBLANES for i in range(SUBLANES)}):
        rolled = x if s == 0 else pltpu.roll(x, s, axis=0)
        if out is None:
            out = rolled
        else:
            sel = functools.reduce(
                jnp.logical_or,
                [row == i for i in range(SUBLANES) if (i - src[i]) % SUBLANES == s])
            out = jnp.where(sel, rolled, out)
    return out


def _hgrn_prep(qb, kb, vb, lgc, rev, row, diag_sel):
    n = SUBLANES
    split = lambda a_: [a_[j * n:(j + 1) * n, :] for j in range(n)]
    q = split(qb.astype(F32))
    kk = split(kb.astype(F32))
    lg = split(lgc)
    order = list(range(n - 1, -1, -1)) if rev else list(range(n))
    w = [None] * n
    acc = None
    for j in order:
        acc = lg[j] if acc is None else acc + lg[j]
        w[j] = acc
    tot = acc
    incl = tot
    for s in (1, 2, 4):
        if not rev:
            incl = incl + jnp.where(row >= s, pltpu.roll(incl, s, axis=0), 0.0)
        else:
            incl = incl + jnp.where(row < n - s, pltpu.roll(incl, n - s, axis=0), 0.0)
    if not rev:
        excl = jnp.where(row >= 1, pltpu.roll(incl, 1, axis=0), 0.0)
        chunk_tot = jnp.broadcast_to(incl[n - 1:n, :], incl.shape)
    else:
        excl = jnp.where(row < n - 1, pltpu.roll(incl, n - 1, axis=0), 0.0)
        chunk_tot = jnp.broadcast_to(incl[0:1, :], incl.shape)
    b = [w[j] + excl for j in range(n)]

    bcat = jnp.concatenate(b, axis=0)

    c = {"vb": vb, "rev": rev}
    c["qe"] = qb * jnp.exp2(bcat).astype(BF16)

    c["lv"] = []
    for m in (1, 2, 4):
        if not rev:
            src = [(i // (2 * m)) * 2 * m + m - 1 for i in range(n)]
        else:
            src = [(i // (2 * m)) * 2 * m + m for i in range(n)]
        y = excl - _gather_rows(incl, src, row)
        e = jnp.concatenate([jnp.exp2(-jnp.abs(w[j] + y)) for j in range(n)], axis=0).astype(BF16)
        c["lv"].append((qb * e, kb * e))

    rows_d = []
    for j in range(n):
        ad = None
        for jp in (range(j + 1) if not rev else range(j, n)):
            prod = q[j] * kk[j] if jp == j else q[j] * kk[jp] * jnp.exp2(w[j] - w[jp])
            red = jnp.broadcast_to(jnp.sum(prod, axis=-1, keepdims=True), (n, HG_CHUNK))
            ad = jnp.where(diag_sel[jp], red, 0.0 if ad is None else ad)
        rows_d.append(ad)
    c["ad"] = jnp.concatenate(rows_d, axis=0)

    tot_cat = jnp.concatenate([chunk_tot] * n, axis=0)
    c["kd"] = kb * jnp.exp2(tot_cat - bcat).astype(BF16)
    c["dec"] = jnp.exp2(chunk_tot[0:1, :])
    return c


def _hgrn_mm1(c, st_ref):
    st = st_ref[...]
    c["st"] = st
    c["o"] = _dot_nt(c["qe"], st.astype(BF16))
    c["am"] = [_dot_nt(qm, km) for qm, km in c["lv"]]
    c["kv"] = _dot_tn(c["vb"], c["kd"])


def _hgrn_mid(c, sel):
    n = SUBLANES
    pieces = []
    for j in range(n):
        rows = slice(j * n, (j + 1) * n)
        a = c["ad"][rows]
        for li, am in enumerate(c["am"]):
            a = jnp.where(sel[(3 if c["rev"] else 0) + li], am[rows], a)
        pieces.append(a)
    c["a"] = jnp.concatenate(pieces, axis=0).astype(BF16)


def _hgrn_mm2(c, st_ref):
    st_ref[...] = c["dec"] * c["st"] + c["kv"]
    return c["o"] + _dot(c["a"], c["vb"])


def _hgrn_kernel(qh_ref, vh_ref, kf_ref, kb_ref, lgf_ref, lgb_ref, gg_ref, gn_ref, masks_ref,
                 out_ref, o_ref, st_ref, *, seq, heads):
    n = SUBLANES
    nchunks = seq // HG_CHUNK
    unroll = max(1, HG_CHAINS // (2 * heads))
    while (nchunks // 2) % unroll:
        unroll //= 2
    row = lax.broadcasted_iota(jnp.int32, (n, HG_DK), 0)
    sel = [masks_ref[i] > 0.5 for i in range(masks_ref.shape[0])]
    col8 = lax.broadcasted_iota(jnp.int32, (n, HG_CHUNK), 1)
    row8 = lax.broadcasted_iota(jnp.int32, (n, HG_CHUNK), 0)
    diag_sel = [col8 == jp * n + row8 for jp in range(n)]
    st_ref[...] = jnp.zeros_like(st_ref)

    def make_body(accumulate):
        def body(it, carry):
            chains = [(hd, d) for hd in range(heads) for d in range(2)]
            ctx, bases = [], []
            for u in range(unroll):
                c = it * unroll + u
                fw = pl.multiple_of(c * HG_CHUNK, HG_CHUNK)
                bw = pl.multiple_of((nchunks - 1 - c) * HG_CHUNK, HG_CHUNK)
                bases.append((fw, bw))
                step = []
                for hd, d in chains:
                    rows = pl.ds(bw if d else fw, HG_CHUNK)
                    cols = slice(hd * HG_DK, (hd + 1) * HG_DK)
                    k_ref, lg_ref = (kb_ref, lgb_ref) if d else (kf_ref, lgf_ref)
                    step.append(_hgrn_prep(qh_ref[rows, cols], k_ref[rows, cols],
                                           vh_ref[rows, cols], lg_ref[rows, cols], d == 1, row,
                                           diag_sel))
                ctx.append(step)
            for u in range(unroll):
                for (hd, d), cx in zip(chains, ctx[u]):
                    _hgrn_mm1(cx, st_ref.at[2 * hd + d])
                for cx in ctx[u]:
                    _hgrn_mid(cx, sel)
                for (hd, d), cx in zip(chains, ctx[u]):
                    o = _hgrn_mm2(cx, st_ref.at[2 * hd + d])
                    base = bases[u][d]
                    for j in range(n):
                        dst = (hd, pl.ds(base + j, n, stride=n), slice(None))
                        piece = o[j * n:(j + 1) * n, :]
                        o_ref[dst] = o_ref[dst] + piece if accumulate else piece
            return carry
        return body

    half_iters = nchunks // (2 * unroll)
    lax.fori_loop(0, half_iters, make_body(False), 0)
    lax.fori_loop(half_iters, 2 * half_iters, make_body(True), 0)

    gn = gn_ref[...]
    rows_e = min(256, seq)

    def epilogue(c, carry):
        sl = pl.ds(pl.multiple_of(c * rows_e, rows_e), rows_e)
        for hd in range(heads):
            cols = slice(hd * HG_DK, (hd + 1) * HG_DK)
            y = _rms(o_ref[hd, sl, :], gn) * gg_ref[sl, cols].astype(F32)
            out_ref[sl, cols] = y.astype(out_ref.dtype)
        return carry

    lax.fori_loop(0, seq // rows_e, epilogue, 0)


def _hgrn(qh, vh, kf, kb, lgf, lgb, gg, gnorm, layer, heads):
    B, L, _ = qh.shape
    masks = _hgrn_consts()
    width = heads * HG_DK
    blk = pl.BlockSpec((None, L, width), lambda b, h: (b, 0, h))
    return pl.pallas_call(
        functools.partial(_hgrn_kernel, seq=L, heads=heads),
        grid=(B, HG_HEADS // heads),
        in_specs=[
            blk, blk, blk, blk, blk, blk, blk,
            _const_spec((None, 1, HG_DK), lambda b, h: (layer, 0, 0)),
            _const_spec(masks.shape, lambda b, h: (0, 0, 0)),
        ],
        out_specs=blk,
        out_shape=jax.ShapeDtypeStruct((B, L, HG_WIDTH), BF16),
        scratch_shapes=[
            pltpu.VMEM((heads, L, HG_DK), F32),
            pltpu.VMEM((2 * heads, HG_DK, HG_DK), F32),
        ],
        compiler_params=pltpu.CompilerParams(
            dimension_semantics=("parallel", "parallel"), vmem_limit_bytes=VMEM_LIMIT),
        name="hgrn2",
    )(qh, vh, kf, kb, lgf, lgb, gg, gnorm, masks)


ATTN_SUB = 128


ATTN_SLACK = 1.0 + 2.0 ** -6
ATTN_MIN_SUM = 2.0 ** -64


def _attn_combine(p, l, lam, v, sw, lambda_init):
    sub, seq = p[0].shape
    w = [jnp.broadcast_to(sc, (sub, LANES)).astype(BF16) for sc in (1.0 / l[0], lam / l[1])]
    a = jnp.concatenate(
        [p[0][:, t * LANES:(t + 1) * LANES] * w[0] - p[1][:, t * LANES:(t + 1) * LANES] * w[1]
         for t in range(seq // LANES)], axis=1)
    return _rms(_dot(a, v), sw) * (1.0 - lambda_init)


def _attn_kernel(q_ref, k_ref, v_ref, lam_ref, sw_ref, out_ref, kaug_ref, kmax_ref,
                 *, lambda_init):
    lp = lam_ref[...]
    lam = (jnp.exp(jnp.sum(lp[0:1] * lp[1:2], axis=-1, keepdims=True))
           - jnp.exp(jnp.sum(lp[2:3] * lp[3:4], axis=-1, keepdims=True)) + lambda_init)
    sw = sw_ref[...]
    tq = q_ref.shape[0]
    seq = k_ref.shape[0]
    sub = min(ATTN_SUB, tq)
    nsub = tq // sub
    spare_lane = (DA_DH, 0)

    def own_lanes(lane, i):
        return (lane < DA_DH) if i == 0 else (lane >= DA_DH)

    @pl.when(pl.program_id(2) == 0)
    def _():
        k = k_ref[...]
        k32 = k.astype(F32)
        sq = k32 * k32
        lane = lax.broadcasted_iota(jnp.int32, k.shape, 1)
        for i in range(2):
            own = own_lanes(lane, i)
            n2 = jnp.sum(jnp.where(own, sq, 0.0), axis=-1, keepdims=True)
            kmax_ref[i] = jnp.broadcast_to(jnp.sqrt(jnp.max(n2, axis=0, keepdims=True)),
                                           kmax_ref.shape[1:])
            kaug_ref[i] = jnp.where(
                own, k32, jnp.where(lane == spare_lane[i], -1.0, 0.0)).astype(BF16)

    qa = []
    for r in range(nsub):
        q32 = q_ref[r * sub:(r + 1) * sub, :].astype(F32)
        sq = q32 * q32
        lane = lax.broadcasted_iota(jnp.int32, q32.shape, 1)
        pair = []
        for i in range(2):
            own = own_lanes(lane, i)
            qn = jnp.sqrt(jnp.sum(jnp.where(own, sq, 0.0), axis=-1, keepdims=True))
            m = qn * kmax_ref[i][0:1, 0:1] * ATTN_SLACK
            pair.append(
                jnp.where(own, q32, jnp.where(lane == spare_lane[i], m, 0.0)).astype(BF16))
        qa.append(pair)
    scores = [[_dot_nt(qa[r][i], kaug_ref[i]) for i in range(2)] for r in range(nsub)]
    lmin = None
    for r in range(nsub):
        p, l = [], []
        for i in range(2):
            e = jnp.exp2(scores[r][i])
            l.append(jnp.sum(e, axis=-1, keepdims=True))
            p.append(e.astype(BF16))
            lmin = l[i] if lmin is None else jnp.minimum(lmin, l[i])
        out_ref[r * sub:(r + 1) * sub, :] = _attn_combine(
            p, l, lam, v_ref[...], sw, lambda_init).astype(out_ref.dtype)
    shifted_ok = jnp.min(lmin) >= ATTN_MIN_SUM

    @pl.when(jnp.logical_not(shifted_ok))
    def _():
        k = k_ref[...]
        for r in range(nsub):
            q = q_ref[r * sub:(r + 1) * sub, :]
            lane = lax.broadcasted_iota(jnp.int32, q.shape, 1)
            zero = jnp.zeros_like(q)
            p, l = [], []
            for i in range(2):
                s = _dot_nt(jnp.where(own_lanes(lane, i), q, zero), k)
                e = jnp.exp2(s - jnp.max(s, axis=-1, keepdims=True))
                l.append(jnp.sum(e, axis=-1, keepdims=True))
                p.append(e.astype(BF16))
            out_ref[r * sub:(r + 1) * sub, :] = _attn_combine(
                p, l, lam, v_ref[...], sw, lambda_init).astype(out_ref.dtype)


def _attention(q3d, k3d, v3d, lam_params, subln, layer, tq):
    B, L, _ = q3d.shape
    lambda_init = 0.8 - 0.6 * math.exp(-0.3 * layer)
    return pl.pallas_call(
        functools.partial(_attn_kernel, lambda_init=lambda_init),
        grid=(B, DA_HEADS, L // tq),
        in_specs=[
            pl.BlockSpec((None, tq, DA_DV), lambda b, h, i: (b, i, h)),
            pl.BlockSpec((None, L, DA_DV), lambda b, h, i: (b, 0, h)),
            pl.BlockSpec((None, L, DA_DV), lambda b, h, i: (b, 0, h)),
            _const_spec((None, 4, DA_DH), lambda b, h, i: (layer, 0, 0)),
            _const_spec((None, 1, DA_DV), lambda b, h, i: (layer, 0, 0)),
        ],
        out_specs=pl.BlockSpec((None, tq, DA_DV), lambda b, h, i: (b, i, h)),
        out_shape=jax.ShapeDtypeStruct((B, L, DA_WIDTH), BF16),
        scratch_shapes=[
            pltpu.VMEM((2, L, DA_DV), BF16),
            pltpu.VMEM((2, SUBLANES, LANES), F32),
        ],
        compiler_params=pltpu.CompilerParams(
            dimension_semantics=("parallel", "parallel", "arbitrary"),
            vmem_limit_bytes=VMEM_LIMIT),
        name="diff_attn",
    )(q3d, k3d, v3d, lam_params, subln)


def _mix_kernel(x_ref, a_ref, b_ref, gate_ref, wa_ref, wb_ref, wo_ref, nw_ref, out_ref):
    ga = _sigmoid(gate_ref[:, :D_MODEL].astype(F32))
    gb = _sigmoid(gate_ref[:, D_MODEL:].astype(F32))
    m = ga * _dot(a_ref[...], wa_ref[...]) + gb * _dot(b_ref[...], wb_ref[...])
    r = _dot(m.astype(BF16), wo_ref[...])
    out_ref[...] = x_ref[...] + _rms(r, nw_ref[...])


def _mix(x2d, a2d, b2d, gate2d, wa, wb, wo, norm_w, layer, tm):
    T = x2d.shape[0]
    return pl.pallas_call(
        _mix_kernel,
        grid=(T // tm,),
        in_specs=[
            pl.BlockSpec((tm, D_MODEL), lambda i: (i, 0)),
            pl.BlockSpec((tm, HG_WIDTH), lambda i: (i, 0)),
            pl.BlockSpec((tm, DA_WIDTH), lambda i: (i, 0)),
            pl.BlockSpec((tm, 2 * D_MODEL), lambda i: (i, 0)),
            _const_spec((None, HG_WIDTH, D_MODEL), lambda i: (layer, 0, 0)),
            _const_spec((None, DA_WIDTH, D_MODEL), lambda i: (layer, 0, 0)),
            _const_spec((None, D_MODEL, D_MODEL), lambda i: (layer, 0, 0)),
            _const_spec((None, 1, D_MODEL), lambda i: (layer, 0, 0)),
        ],
        out_specs=pl.BlockSpec((tm, D_MODEL), lambda i: (i, 0)),
        out_shape=jax.ShapeDtypeStruct((T, D_MODEL), F32),
        compiler_params=pltpu.CompilerParams(
            dimension_semantics=("parallel",), vmem_limit_bytes=VMEM_LIMIT),
        name="mix_out",
    )(x2d, a2d, b2d, gate2d, wa, wb, wo, norm_w)


FFN_HALO = 8
FFN_TF = 256


GELU_C = math.sqrt(2.0 / math.pi)
GELU_A = 0.044715


def _ffn_kernel(x_ref, xp_ref, xn_ref, p_ref, nwpre_ref, wup_ref, cw_ref, cb_ref, wdn_ref,
                nwpost_ref, wple_ref, wpg_ref, nwple_ref, out_ref, h_ref, slab_ref, act_ref,
                *, tm, tiles_per_seq):
    i = pl.program_id(0)
    pos = i % tiles_per_seq
    nw = nwpre_ref[...]
    x = x_ref[...]
    n = SUBLANES
    nch = tm // HG_CHUNK
    e = _dot(p_ref[...].astype(BF16), wple_ref[...])
    _to_slabs(slab_ref, _rms(x, nw))
    h_ref[0:tm, :] = _chunk_transposed(slab_ref, tm).astype(BF16)
    hp = _rms(xp_ref[...], nw) * jnp.where(pos == 0, 0.0, 1.0)
    hn = _rms(xn_ref[...], nw) * jnp.where(pos == tiles_per_seq - 1, 0.0, 1.0)
    h_ref[tm:, :] = jnp.concatenate([hp, hn], axis=0).astype(BF16)
    row = lax.broadcasted_iota(jnp.int32, (nch, n, FFN_TF), 1)

    def conv(u, col, scale):
        body = u[0:tm].reshape(nch, n, n, FFN_TF)
        before = u[tm:tm + n]
        after = u[tm + n:tm + 2 * n]
        first, last = body[:, 0], body[:, n - 1]
        prevs = jnp.concatenate([before[None], last[:-1]], axis=0)
        nexts = jnp.concatenate([first[1:], after[None]], axis=0)
        dn0 = jnp.where(row == 0, pltpu.roll(prevs, 1, axis=1), pltpu.roll(last, 1, axis=1))
        up7 = jnp.where(row == n - 1, pltpu.roll(nexts, n - 1, axis=1),
                        pltpu.roll(first, n - 1, axis=1))
        dn = jnp.concatenate([dn0[:, None], body[:, :n - 1]], axis=1)
        up = jnp.concatenate([body[:, 1:], up7[:, None]], axis=1)
        cw = cw_ref[:, col] * scale
        return dn * cw[0:1] + body * cw[1:2] + up * cw[2:3] + cb_ref[:, col] * scale

    for c in range(D_FF // FFN_TF):
        gcol = slice(c * FFN_TF, (c + 1) * FFN_TF)
        vcol = slice(D_FF + c * FFN_TF, D_FF + (c + 1) * FFN_TF)
        h = h_ref[...]
        g = conv(_dot(h, wup_ref[:, gcol]), gcol, 1.0)
        half_val = conv(_dot(h, wup_ref[:, vcol]), vcol, 0.5)
        t = jnp.tanh(g * (GELU_C + (GELU_C * GELU_A) * (g * g)))
        act_ref[:, gcol] = (g * (1.0 + t) * half_val).reshape(tm, FFN_TF).astype(BF16)

    nblk = 2 if nch % 2 == 0 else 1
    rb = tm // nblk
    blocks = [slice(b * rb, (b + 1) * rb) for b in range(nblk)]
    f = [_dot(act_ref[rows, :], wdn_ref[...]) for rows in blocks]
    x2, g = [], []
    for b, rows in enumerate(blocks):
        _to_slabs(slab_ref, f[b], b * rb)
        x2.append(x[rows] + _rms(_chunk_transposed(slab_ref, rb, b * rb), nwpost_ref[...]))
        g.append(_dot(_rms(x2[b]).astype(BF16), wpg_ref[...]))
    for b, rows in enumerate(blocks):
        out_ref[rows, :] = x2[b] + _rms(_sigmoid(g[b]) * e[rows], nwple_ref[...])


def _ffn_ple(x2d, p2d, nw_pre, w_up, conv_w, conv_b, w_down, nw_post,
             w_ple, w_ple_gate, nw_ple, layer, seq, tm):
    T = x2d.shape[0]
    tiles_per_seq = seq // tm
    hb = tm // FFN_HALO
    nhalo = T // FFN_HALO

    def lw(shape):
        return _const_spec((None,) + shape, lambda i: (layer,) + (0,) * len(shape))

    return pl.pallas_call(
        functools.partial(_ffn_kernel, tm=tm, tiles_per_seq=tiles_per_seq),
        grid=(T // tm,),
        in_specs=[
            pl.BlockSpec((tm, D_MODEL), lambda i: (i, 0)),
            pl.BlockSpec((FFN_HALO, D_MODEL), lambda i: (jnp.maximum(i * hb - 1, 0), 0)),
            pl.BlockSpec((FFN_HALO, D_MODEL), lambda i: (jnp.minimum((i + 1) * hb, nhalo - 1), 0)),
            pl.BlockSpec((None, tm, PLE_DIM), lambda i: (layer, i, 0)),
            lw((1, D_MODEL)),
            lw((D_MODEL, 2 * D_FF)),
            lw((3, 2 * D_FF)),
            lw((1, 2 * D_FF)),
            lw((D_FF, D_MODEL)),
            lw((1, D_MODEL)),
            lw((PLE_DIM, D_MODEL)),
            lw((D_MODEL, D_MODEL)),
            lw((1, D_MODEL)),
        ],
        out_specs=pl.BlockSpec((tm, D_MODEL), lambda i: (i, 0)),
        out_shape=jax.ShapeDtypeStruct((T, D_MODEL), F32),
        scratch_shapes=[
            pltpu.VMEM((tm + 2 * FFN_HALO, D_MODEL), BF16),
            pltpu.VMEM((D_MODEL // LANES, tm, LANES), F32),
            pltpu.VMEM((tm, D_FF), BF16),
        ],
        compiler_params=pltpu.CompilerParams(
            dimension_semantics=("parallel",), vmem_limit_bytes=VMEM_LIMIT),
        name="convffn_ple",
    )(x2d, x2d, x2d, p2d, nw_pre, w_up, conv_w, conv_b, w_down, nw_post,
      w_ple, w_ple_gate, nw_ple)


def _rope_tables(L):
    inv = 1.0 / (ROPE_THETA ** (jnp.arange(0, DA_DH, 2, dtype=F32) / DA_DH))
    ang = jnp.arange(L, dtype=F32)[:, None] * inv[None, :]
    cos = jnp.tile(jnp.cos(ang), (1, 4))
    sin = jnp.sin(ang)
    sin_signed = jnp.tile(jnp.concatenate([-sin, sin], axis=-1), (1, 2))
    return cos, sin_signed


def _trunk(x, p, w, tm_in, tm_mix, tm_ffn, tq, hg_heads=2):
    B, L, _ = x.shape
    T = B * L
    cos, sin = _rope_tables(L)
    x2d = x.reshape(T, D_MODEL)
    p3d = p.reshape(DEPTH, T, PLE_DIM)
    for l in range(DEPTH):
        (qh, vh, kf, kb, lgf, lgb, gg, q, k, v, gate) = _in_proj(
            x2d, w["norm_mix_pre"], w["w_in"], cos, sin, w["lb_logits"], l, L, tm_in)
        seq3 = lambda t: t.reshape(B, L, HG_WIDTH)
        a = _hgrn(seq3(qh), seq3(vh), seq3(kf), seq3(kb), seq3(lgf), seq3(lgb), seq3(gg),
                  w["hgrn_gnorm"], l, hg_heads)
        bb = _attention(q.reshape(B, L, DA_WIDTH), k.reshape(B, L, DA_WIDTH),
                        v.reshape(B, L, DA_WIDTH), w["diff_lambda"], w["diff_subln"], l, tq)
        x2d = _mix(x2d, a.reshape(T, HG_WIDTH), bb.reshape(T, DA_WIDTH), gate,
                   w["w_branch_a"], w["w_branch_b"], w["w_out"], w["norm_mix_post"], l, tm_mix)
        x2d = _ffn_ple(x2d, p3d, w["norm_ffn_pre"], w["w_up"], w["conv_w"], w["conv_b"],
                       w["w_down"], w["norm_ffn_post"], w["w_ple"], w["w_ple_gate"],
                       w["norm_ple"], l, L, tm_ffn)
    return x2d.reshape(B, L, D_MODEL)


def _prep_weights(w_in, hgrn_lb_logits, hgrn_gnorm, diff_lambda, diff_subln, w_branch_a,
                  w_branch_b, w_out, norm_mix_pre, norm_mix_post, w_up, conv_w, conv_b, w_down,
                  norm_ffn_pre, norm_ffn_post, w_ple, w_ple_gate, norm_ple):
    def vec(a):
        return a.reshape(DEPTH, 1, a.shape[-1]).astype(F32)

    return {
        "w_in": w_in.astype(BF16),
        "lb_logits": hgrn_lb_logits.reshape(2 * DEPTH, HG_WIDTH).astype(F32),
        "hgrn_gnorm": vec(hgrn_gnorm),
        "diff_lambda": diff_lambda.astype(F32),
        "diff_subln": vec(diff_subln),
        "w_branch_a": w_branch_a.astype(BF16),
        "w_branch_b": w_branch_b.astype(BF16),
        "w_out": w_out.astype(BF16),
        "norm_mix_pre": vec(norm_mix_pre),
        "norm_mix_post": vec(norm_mix_post),
        "w_up": w_up.astype(BF16),
        "conv_w": conv_w.astype(F32),
        "conv_b": vec(conv_b),
        "w_down": w_down.astype(BF16),
        "norm_ffn_pre": vec(norm_ffn_pre),
        "norm_ffn_post": vec(norm_ffn_post),
        "w_ple": w_ple.astype(BF16),
        "w_ple_gate": w_ple_gate.astype(BF16),
        "norm_ple": vec(norm_ple),
    }


def _tile(n, pref):
    t = min(n, pref)
    assert n % t == 0
    return t


def kernel(x_prompt, x_sample, p_prompt, p_sample, w_in, hgrn_lb_logits, hgrn_gnorm, diff_lambda, diff_subln, w_branch_a, w_branch_b, w_out, norm_mix_pre, norm_mix_post, w_up, conv_w, conv_b, w_down, norm_ffn_pre, norm_ffn_post, w_ple, w_ple_gate, norm_ple):
    w = _prep_weights(w_in, hgrn_lb_logits, hgrn_gnorm, diff_lambda, diff_subln, w_branch_a,
                      w_branch_b, w_out, norm_mix_pre, norm_mix_post, w_up, conv_w, conv_b,
                      w_down, norm_ffn_pre, norm_ffn_post, w_ple, w_ple_gate, norm_ple)
    outs = []
    for x, p in ((x_prompt, p_prompt), (x_sample, p_sample)):
        L = x.shape[1]
        hg_heads = HG_HEADS if L <= 2048 else HG_HEADS // 2
        outs.append(_trunk(x, p, w, tm_in=_tile(L, 512), tm_mix=_tile(L, 1024),
                           tm_ffn=_tile(L, 512), tq=_tile(L, 512), hg_heads=hg_heads))
    return tuple(outs)
```

```python
import functools
import math

import jax
import jax.numpy as jnp
from jax import lax
from jax.experimental import pallas as pl
from jax.experimental.pallas import tpu as pltpu

D_MODEL = 1024
DEPTH = 4
PLE_DIM = 256
HG_WIDTH = D_MODEL // 2
HG_HEADS = 4
HG_DK = HG_WIDTH // HG_HEADS
LB_FLOOR = 1e-30
DA_WIDTH = D_MODEL // 2
DA_HEADS = 4
DA_DV = DA_WIDTH // DA_HEADS
DA_DH = DA_DV // 2
D_FF = 2816
ROPE_THETA = 10000.0
EPS = 1e-6
IN_COLS = 5 * HG_WIDTH + 3 * DA_WIDTH + 2 * D_MODEL

LANES = 128
SUBLANES = 8
HG_CHUNK = SUBLANES * SUBLANES
HG_CHAINS = 32
VMEM_LIMIT = 56 * 1024 * 1024

BF16 = jnp.bfloat16
F32 = jnp.float32
LOG2E = math.log2(math.e)


def _dot(a, b):
    return jnp.dot(a, b, preferred_element_type=F32)


def _dot_nt(a, b):
    return lax.dot_general(a, b, (((1,), (1,)), ((), ())), preferred_element_type=F32)


def _dot_tn(a, b):
    return lax.dot_general(a, b, (((0,), (0,)), ((), ())), preferred_element_type=F32)


def _rms(x, w=None):
    y = x * lax.rsqrt(jnp.mean(x * x, axis=-1, keepdims=True) + EPS)
    return y if w is None else y * w


def _sigmoid(x):
    return 1.0 / (1.0 + jnp.exp(-x))


def _const_spec(block_shape, index_map):
    return pl.BlockSpec(block_shape, index_map, pipeline_mode=pl.Buffered(1))


def _to_slabs(slab_ref, val, row0=0):
    for s in range(val.shape[1] // LANES):
        slab_ref[s, row0:row0 + val.shape[0], :] = val[:, s * LANES:(s + 1) * LANES]


def _chunk_transposed(slab_ref, rows, row0=0):
    n = SUBLANES
    return jnp.concatenate(
        [jnp.concatenate([slab_ref[s, pl.ds(row0 + c * HG_CHUNK + j, n, stride=n), :]
                          for s in range(slab_ref.shape[0])], axis=1)
         for c in range(rows // HG_CHUNK) for j in range(n)], axis=0)


def _rope(z, cos, sin_signed, first_half):
    rot = jnp.where(first_half, pltpu.roll(z, 96, axis=1), pltpu.roll(z, 32, axis=1))
    return z * cos + rot * sin_signed


def _lower_bounds(lbl_ref, layer):
    out = []
    for d in range(2):
        rows = [lbl_ref[2 * l + d:2 * l + d + 1, :] for l in range(DEPTH)]
        mx = functools.reduce(jnp.maximum, rows)
        ex = [jnp.exp(r - mx) for r in rows]
        den = functools.reduce(lambda a_, b_: a_ + b_, ex)
        sm = [e / den for e in ex]
        cs = sm[0]
        for l in range(1, layer + 1):
            cs = cs + sm[l]
        out.append(cs - sm[0])
    return out


def _hgrn_gates(z, lb):
    t = jnp.exp(-jnp.abs(z))
    r = 1.0 / (1.0 + t)
    tr = t * r
    pos = z > 0
    sig_pos = jnp.where(pos, r, tr)
    sig_neg = jnp.where(pos, tr, r)
    one_m_lb = 1.0 - lb
    f = jnp.maximum(lb, LB_FLOOR) + one_m_lb * sig_pos
    return one_m_lb * sig_neg, jnp.log2(f)


def _in_proj_kernel(x_ref, nw_ref, w_ref, cos_ref, sin_ref, lbl_ref,
                    qh_ref, vh_ref, kf_ref, kb_ref, lgf_ref, lgb_ref, gg_ref,
                    q_ref, k_ref, v_ref, gate_ref, hn_ref, *, layer, tm):
    hn = _rms(x_ref[...], nw_ref[...])
    h = hn.astype(BF16)
    _to_slabs(hn_ref, hn)
    hp = _chunk_transposed(hn_ref, tm).astype(BF16)
    cw = HG_WIDTH
    lb = _lower_bounds(lbl_ref, layer)
    cos = cos_ref[...]
    sin = sin_ref[...]
    lane = lax.broadcasted_iota(jnp.int32, cos.shape, 1)
    first_half = (lane & (DA_DH - 1)) < (DA_DH // 2)
    base = 5 * HG_WIDTH
    gbase = base + 3 * DA_WIDTH

    def col(lhs, c):
        return _dot(lhs, w_ref[:, c * cw:(c + 1) * cw])

    def merge_gate(c):
        gate_ref[:, c * cw:(c + 1) * cw] = col(h, gbase // cw + c).astype(BF16)

    def rope_out(out_ref, z, scale):
        for hd in range(DA_HEADS):
            sl = slice(hd * LANES, (hd + 1) * LANES)
            out_ref[:, sl] = (_rope(z[:, sl], cos, sin, first_half) * scale).astype(BF16)

    zq_d = col(h, base // cw)
    zk_d = col(h, base // cw + 1)
    rope_out(q_ref, zq_d, LOG2E * DA_DH ** -0.5)
    rope_out(k_ref, zk_d, 1.0)
    zq = col(hp, 0)
    qh_ref[...] = (zq * _sigmoid(zq) * (HG_DK ** -0.5)).astype(BF16)
    kf, lgf = _hgrn_gates(col(hp, 1), lb[0])
    kf_ref[...] = kf.astype(BF16)
    lgf_ref[...] = lgf
    kb, lgb = _hgrn_gates(col(hp, 2), lb[1])
    kb_ref[...] = kb.astype(BF16)
    lgb_ref[...] = lgb
    vh_ref[...] = col(hp, 3).astype(BF16)
    zg = col(h, 4)
    gg_ref[...] = (zg * _sigmoid(zg)).astype(BF16)
    v_ref[...] = col(h, base // cw + 2).astype(BF16)
    for c in range(4):
        merge_gate(c)


def _in_proj(x2d, norm_w, w_in, cos, sin, lb_logits2d, layer, seq, tm):
    T = x2d.shape[0]
    npos = seq // tm
    half = lambda dt: (pl.BlockSpec((tm, HG_WIDTH), lambda i: (i, 0)),
                       jax.ShapeDtypeStruct((T, HG_WIDTH), dt))
    outs = [half(BF16), half(BF16), half(BF16), half(BF16), half(F32), half(F32), half(BF16),
            half(BF16), half(BF16), half(BF16),
            (pl.BlockSpec((tm, 2 * D_MODEL), lambda i: (i, 0)),
             jax.ShapeDtypeStruct((T, 2 * D_MODEL), BF16))]
    return pl.pallas_call(
        functools.partial(_in_proj_kernel, layer=layer, tm=tm),
        grid=(T // tm,),
        in_specs=[
            pl.BlockSpec((tm, D_MODEL), lambda i: (i, 0)),
            _const_spec((None, 1, D_MODEL), lambda i: (layer, 0, 0)),
            _const_spec((None, D_MODEL, IN_COLS), lambda i: (layer, 0, 0)),
            pl.BlockSpec((tm, LANES), lambda i: (i % npos, 0)),
            pl.BlockSpec((tm, LANES), lambda i: (i % npos, 0)),
            _const_spec((2 * DEPTH, HG_WIDTH), lambda i: (0, 0)),
        ],
        out_specs=[o[0] for o in outs],
        out_shape=[o[1] for o in outs],
        scratch_shapes=[pltpu.VMEM((D_MODEL // LANES, tm, LANES), F32)],
        compiler_params=pltpu.CompilerParams(
            dimension_semantics=("parallel",), vmem_limit_bytes=VMEM_LIMIT),
        name="in_proj",
    )(x2d, norm_w, w_in, cos, sin, lb_logits2d)


def _hgrn_consts():
    blk = jnp.arange(SUBLANES)[:, None]
    cblk = (jnp.arange(HG_CHUNK) % SUBLANES)[None, :]
    masks = []
    for rev in (False, True):
        for m in (1, 2, 4):
            same = (blk // (2 * m)) == (cblk // (2 * m))
            r_hi = (blk % (2 * m)) >= m
            c_hi = (cblk % (2 * m)) >= m
            mk = same & (r_hi & ~c_hi if not rev else ~r_hi & c_hi)
            masks.append(mk)
    return jnp.stack(masks).astype(F32)


def _gather_rows(x, src, row):
    out = None
    for s in sorted({(i - src[i]) % SUBLANES for i in range(SUBLANES)}):
        rolled = x if s == 0 else pltpu.roll(x, s, axis=0)
        if out is None:
            out = rolled
        else:
            sel = functools.reduce(
                jnp.logical_or,
                [row == i for i in range(SUBLANES) if (i - src[i]) % SUBLANES == s])
            out = jnp.where(sel, rolled, out)
    return out


def _pair_products(q, kk, w, rev):
    n = len(q)
    prods = {(j, j): q[j] * kk[j] for j in range(n)}

    def split(lo, hi):
        if hi - lo < 2:
            return
        mid = (lo + hi) // 2
        early, late = range(lo, mid), range(mid, hi)
        rows, cols, ref = (late, early, mid - 1) if not rev else (early, late, mid)
        if hi - lo == 2:
            (j,), (jp,) = rows, cols
            prods[(j, jp)] = q[j] * kk[jp] * jnp.exp2(w[j] - w[jp])
        else:
            qs = {j: q[j] * jnp.exp2(w[j] - w[ref]) for j in rows}
            ks = {jp: kk[jp] if jp == ref else kk[jp] * jnp.exp2(w[ref] - w[jp]) for jp in cols}
            for j in rows:
                for jp in cols:
                    prods[(j, jp)] = qs[j] * ks[jp]
        split(lo, mid)
        split(mid, hi)

    split(0, n)
    return prods


def _hgrn_prep(qb, kb, vb, lgc, rev, row, diag_sel):
    n = SUBLANES
    split = lambda a_: [a_[j * n:(j + 1) * n, :] for j in range(n)]
    q = split(qb.astype(F32))
    kk = split(kb.astype(F32))
    lg = split(lgc)
    order = list(range(n - 1, -1, -1)) if rev else list(range(n))
    w = [None] * n
    acc = None
    for j in order:
        acc = lg[j] if acc is None else acc + lg[j]
        w[j] = acc
    tot = acc
    incl = tot
    for s in (1, 2, 4):
        if not rev:
            incl = incl + jnp.where(row >= s, pltpu.roll(incl, s, axis=0), 0.0)
        else:
            incl = incl + jnp.where(row < n - s, pltpu.roll(incl, n - s, axis=0), 0.0)
    if not rev:
        excl = jnp.where(row >= 1, pltpu.roll(incl, 1, axis=0), 0.0)
        chunk_tot = jnp.broadcast_to(incl[n - 1:n, :], incl.shape)
    else:
        excl = jnp.where(row < n - 1, pltpu.roll(incl, n - 1, axis=0), 0.0)
        chunk_tot = jnp.broadcast_to(incl[0:1, :], incl.shape)
    b = [w[j] + excl for j in range(n)]

    bcat = jnp.concatenate(b, axis=0)

    c = {"vb": vb, "rev": rev}
    c["qe"] = qb * jnp.exp2(bcat).astype(BF16)

    c["lv"] = []
    for m in (1, 2, 4):
        if not rev:
            src = [(i // (2 * m)) * 2 * m + m - 1 for i in range(n)]
        else:
            src = [(i // (2 * m)) * 2 * m + m for i in range(n)]
        y = excl - _gather_rows(incl, src, row)
        e = jnp.concatenate([jnp.exp2(-jnp.abs(w[j] + y)) for j in range(n)], axis=0).astype(BF16)
        c["lv"].append((qb * e, kb * e))

    prods = _pair_products(q, kk, w, rev)
    rows_d = []
    for j in range(n):
        ad = None
        for jp in (range(j + 1) if not rev else range(j, n)):
            red = jnp.broadcast_to(jnp.sum(prods[(j, jp)], axis=-1, keepdims=True),
                                   (n, HG_CHUNK))
            ad = jnp.where(diag_sel[jp], red, 0.0 if ad is None else ad)
        rows_d.append(ad)
    c["ad"] = jnp.concatenate(rows_d, axis=0)

    tot_cat = jnp.concatenate([chunk_tot] * n, axis=0)
    c["kd"] = kb * jnp.exp2(tot_cat - bcat).astype(BF16)
    c["dec"] = jnp.exp2(chunk_tot[0:1, :])
    return c


def _hgrn_mm1(c, st_ref):
    st = st_ref[...]
    c["st"] = st
    c["o"] = _dot_nt(c["qe"], st.astype(BF16))
    c["am"] = [_dot_nt(qm, km) for qm, km in c["lv"]]
    c["kv"] = _dot_tn(c["vb"], c["kd"])


def _hgrn_mid(c, sel):
    n = SUBLANES
    pieces = []
    for j in range(n):
        rows = slice(j * n, (j + 1) * n)
        a = c["ad"][rows]
        for li, am in enumerate(c["am"]):
            a = jnp.where(sel[(3 if c["rev"] else 0) + li], am[rows], a)
        pieces.append(a)
    c["a"] = jnp.concatenate(pieces, axis=0).astype(BF16)


def _hgrn_mm2(c, st_ref):
    st_ref[...] = c["dec"] * c["st"] + c["kv"]
    return c["o"] + _dot(c["a"], c["vb"])


def _hgrn_kernel(qh_ref, vh_ref, kf_ref, kb_ref, lgf_ref, lgb_ref, gg_ref, gn_ref, masks_ref,
                 out_ref, o_ref, st_ref, *, seq, heads):
    n = SUBLANES
    nchunks = seq // HG_CHUNK
    unroll = max(1, HG_CHAINS // (2 * heads))
    while (nchunks // 2) % unroll:
        unroll //= 2
    row = lax.broadcasted_iota(jnp.int32, (n, HG_DK), 0)
    sel = [masks_ref[i] > 0.5 for i in range(masks_ref.shape[0])]
    col8 = lax.broadcasted_iota(jnp.int32, (n, HG_CHUNK), 1)
    row8 = lax.broadcasted_iota(jnp.int32, (n, HG_CHUNK), 0)
    diag_sel = [col8 == jp * n + row8 for jp in range(n)]
    st_ref[...] = jnp.zeros_like(st_ref)

    def make_body(accumulate):
        def body(it, carry):
            chains = [(hd, d) for hd in range(heads) for d in range(2)]
            ctx, bases = [], []
            for u in range(unroll):
                c = it * unroll + u
                fw = pl.multiple_of(c * HG_CHUNK, HG_CHUNK)
                bw = pl.multiple_of((nchunks - 1 - c) * HG_CHUNK, HG_CHUNK)
                bases.append((fw, bw))
                step = []
                for hd, d in chains:
                    rows = pl.ds(bw if d else fw, HG_CHUNK)
                    cols = slice(hd * HG_DK, (hd + 1) * HG_DK)
                    k_ref, lg_ref = (kb_ref, lgb_ref) if d else (kf_ref, lgf_ref)
                    step.append(_hgrn_prep(qh_ref[rows, cols], k_ref[rows, cols],
                                           vh_ref[rows, cols], lg_ref[rows, cols], d == 1, row,
                                           diag_sel))
                ctx.append(step)
            for u in range(unroll):
                for (hd, d), cx in zip(chains, ctx[u]):
                    _hgrn_mm1(cx, st_ref.at[2 * hd + d])
                for cx in ctx[u]:
                    _hgrn_mid(cx, sel)
                for (hd, d), cx in zip(chains, ctx[u]):
                    o = _hgrn_mm2(cx, st_ref.at[2 * hd + d])
                    base = bases[u][d]
                    for j in range(n):
                        dst = (hd, pl.ds(base + j, n, stride=n), slice(None))
                        piece = o[j * n:(j + 1) * n, :]
                        o_ref[dst] = o_ref[dst] + piece if accumulate else piece
            return carry
        return body

    half_iters = nchunks // (2 * unroll)
    lax.fori_loop(0, half_iters, make_body(False), 0)
    lax.fori_loop(half_iters, 2 * half_iters, make_body(True), 0)

    gn = gn_ref[...]
    rows_e = min(256, seq)

    def epilogue(c, carry):
        sl = pl.ds(pl.multiple_of(c * rows_e, rows_e), rows_e)
        for hd in range(heads):
            cols = slice(hd * HG_DK, (hd + 1) * HG_DK)
            y = _rms(o_ref[hd, sl, :], gn) * gg_ref[sl, cols].astype(F32)
            out_ref[sl, cols] = y.astype(out_ref.dtype)
        return carry

    lax.fori_loop(0, seq // rows_e, epilogue, 0)


def _hgrn(qh, vh, kf, kb, lgf, lgb, gg, gnorm, layer, heads):
    B, L, _ = qh.shape
    masks = _hgrn_consts()
    width = heads * HG_DK
    blk = pl.BlockSpec((None, L, width), lambda b, h: (b, 0, h))
    return pl.pallas_call(
        functools.partial(_hgrn_kernel, seq=L, heads=heads),
        grid=(B, HG_HEADS // heads),
        in_specs=[
            blk, blk, blk, blk, blk, blk, blk,
            _const_spec((None, 1, HG_DK), lambda b, h: (layer, 0, 0)),
            _const_spec(masks.shape, lambda b, h: (0, 0, 0)),
        ],
        out_specs=blk,
        out_shape=jax.ShapeDtypeStruct((B, L, HG_WIDTH), BF16),
        scratch_shapes=[
            pltpu.VMEM((heads, L, HG_DK), F32),
            pltpu.VMEM((2 * heads, HG_DK, HG_DK), F32),
        ],
        compiler_params=pltpu.CompilerParams(
            dimension_semantics=("parallel", "parallel"), vmem_limit_bytes=VMEM_LIMIT),
        name="hgrn2",
    )(qh, vh, kf, kb, lgf, lgb, gg, gnorm, masks)


ATTN_SUB = 128


ATTN_SLACK = 1.0 + 2.0 ** -6
ATTN_MIN_SUM = 2.0 ** -64


def _attn_combine(p, l, lam, v, sw, lambda_init):
    sub, seq = p[0].shape
    w = [jnp.broadcast_to(sc, (sub, LANES)).astype(BF16) for sc in (1.0 / l[0], lam / l[1])]
    a = jnp.concatenate(
        [p[0][:, t * LANES:(t + 1) * LANES] * w[0] - p[1][:, t * LANES:(t + 1) * LANES] * w[1]
         for t in range(seq // LANES)], axis=1)
    return _rms(_dot(a, v), sw) * (1.0 - lambda_init)


def _attn_kernel(q_ref, k_ref, v_ref, lam_ref, sw_ref, out_ref, kaug_ref, kmax_ref,
                 *, lambda_init):
    lp = lam_ref[...]
    lam = (jnp.exp(jnp.sum(lp[0:1] * lp[1:2], axis=-1, keepdims=True))
           - jnp.exp(jnp.sum(lp[2:3] * lp[3:4], axis=-1, keepdims=True)) + lambda_init)
    sw = sw_ref[...]
    tq = q_ref.shape[0]
    seq = k_ref.shape[0]
    sub = min(ATTN_SUB, tq)
    nsub = tq // sub
    spare_lane = (DA_DH, 0)

    def own_lanes(lane, i):
        return (lane < DA_DH) if i == 0 else (lane >= DA_DH)

    @pl.when(pl.program_id(2) == 0)
    def _():
        k = k_ref[...]
        k32 = k.astype(F32)
        sq = k32 * k32
        lane = lax.broadcasted_iota(jnp.int32, k.shape, 1)
        for i in range(2):
            own = own_lanes(lane, i)
            n2 = jnp.sum(jnp.where(own, sq, 0.0), axis=-1, keepdims=True)
            kmax_ref[i] = jnp.broadcast_to(jnp.sqrt(jnp.max(n2, axis=0, keepdims=True)),
                                           kmax_ref.shape[1:])
            kaug_ref[i] = jnp.where(
                own, k32, jnp.where(lane == spare_lane[i], -1.0, 0.0)).astype(BF16)

    qa = []
    for r in range(nsub):
        q32 = q_ref[r * sub:(r + 1) * sub, :].astype(F32)
        sq = q32 * q32
        lane = lax.broadcasted_iota(jnp.int32, q32.shape, 1)
        pair = []
        for i in range(2):
            own = own_lanes(lane, i)
            qn = jnp.sqrt(jnp.sum(jnp.where(own, sq, 0.0), axis=-1, keepdims=True))
            m = qn * kmax_ref[i][0:1, 0:1] * ATTN_SLACK
            pair.append(
                jnp.where(own, q32, jnp.where(lane == spare_lane[i], m, 0.0)).astype(BF16))
        qa.append(pair)
    scores = [[_dot_nt(qa[r][i], kaug_ref[i]) for i in range(2)] for r in range(nsub)]
    lmin = None
    for r in range(nsub):
        p, l = [], []
        for i in range(2):
            e = jnp.exp2(scores[r][i])
            l.append(jnp.sum(e, axis=-1, keepdims=True))
            p.append(e.astype(BF16))
            lmin = l[i] if lmin is None else jnp.minimum(lmin, l[i])
        out_ref[r * sub:(r + 1) * sub, :] = _attn_combine(
            p, l, lam, v_ref[...], sw, lambda_init).astype(out_ref.dtype)
    shifted_ok = jnp.min(lmin) >= ATTN_MIN_SUM

    @pl.when(jnp.logical_not(shifted_ok))
    def _():
        k = k_ref[...]
        for r in range(nsub):
            q = q_ref[r * sub:(r + 1) * sub, :]
            lane = lax.broadcasted_iota(jnp.int32, q.shape, 1)
            zero = jnp.zeros_like(q)
            p, l = [], []
            for i in range(2):
                s = _dot_nt(jnp.where(own_lanes(lane, i), q, zero), k)
                e = jnp.exp2(s - jnp.max(s, axis=-1, keepdims=True))
                l.append(jnp.sum(e, axis=-1, keepdims=True))
                p.append(e.astype(BF16))
            out_ref[r * sub:(r + 1) * sub, :] = _attn_combine(
                p, l, lam, v_ref[...], sw, lambda_init).astype(out_ref.dtype)


def _attention(q3d, k3d, v3d, lam_params, subln, layer, tq):
    B, L, _ = q3d.shape
    lambda_init = 0.8 - 0.6 * math.exp(-0.3 * layer)
    return pl.pallas_call(
        functools.partial(_attn_kernel, lambda_init=lambda_init),
        grid=(B, DA_HEADS, L // tq),
        in_specs=[
            pl.BlockSpec((None, tq, DA_DV), lambda b, h, i: (b, i, h)),
            pl.BlockSpec((None, L, DA_DV), lambda b, h, i: (b, 0, h)),
            pl.BlockSpec((None, L, DA_DV), lambda b, h, i: (b, 0, h)),
            _const_spec((None, 4, DA_DH), lambda b, h, i: (layer, 0, 0)),
            _const_spec((None, 1, DA_DV), lambda b, h, i: (layer, 0, 0)),
        ],
        out_specs=pl.BlockSpec((None, tq, DA_DV), lambda b, h, i: (b, i, h)),
        out_shape=jax.ShapeDtypeStruct((B, L, DA_WIDTH), BF16),
        scratch_shapes=[
            pltpu.VMEM((2, L, DA_DV), BF16),
            pltpu.VMEM((2, SUBLANES, LANES), F32),
        ],
        compiler_params=pltpu.CompilerParams(
            dimension_semantics=("parallel", "parallel", "arbitrary"),
            vmem_limit_bytes=VMEM_LIMIT),
        name="diff_attn",
    )(q3d, k3d, v3d, lam_params, subln)


def _mix_kernel(x_ref, a_ref, b_ref, gate_ref, wa_ref, wb_ref, wo_ref, nw_ref, out_ref):
    ga = _sigmoid(gate_ref[:, :D_MODEL].astype(F32))
    gb = _sigmoid(gate_ref[:, D_MODEL:].astype(F32))
    m = ga * _dot(a_ref[...], wa_ref[...]) + gb * _dot(b_ref[...], wb_ref[...])
    r = _dot(m.astype(BF16), wo_ref[...])
    out_ref[...] = x_ref[...] + _rms(r, nw_ref[...])


def _mix(x2d, a2d, b2d, gate2d, wa, wb, wo, norm_w, layer, tm):
    T = x2d.shape[0]
    return pl.pallas_call(
        _mix_kernel,
        grid=(T // tm,),
        in_specs=[
            pl.BlockSpec((tm, D_MODEL), lambda i: (i, 0)),
            pl.BlockSpec((tm, HG_WIDTH), lambda i: (i, 0)),
            pl.BlockSpec((tm, DA_WIDTH), lambda i: (i, 0)),
            pl.BlockSpec((tm, 2 * D_MODEL), lambda i: (i, 0)),
            _const_spec((None, HG_WIDTH, D_MODEL), lambda i: (layer, 0, 0)),
            _const_spec((None, DA_WIDTH, D_MODEL), lambda i: (layer, 0, 0)),
            _const_spec((None, D_MODEL, D_MODEL), lambda i: (layer, 0, 0)),
            _const_spec((None, 1, D_MODEL), lambda i: (layer, 0, 0)),
        ],
        out_specs=pl.BlockSpec((tm, D_MODEL), lambda i: (i, 0)),
        out_shape=jax.ShapeDtypeStruct((T, D_MODEL), F32),
        compiler_params=pltpu.CompilerParams(
            dimension_semantics=("parallel",), vmem_limit_bytes=VMEM_LIMIT),
        name="mix_out",
    )(x2d, a2d, b2d, gate2d, wa, wb, wo, norm_w)


FFN_HALO = 8
FFN_TF = 256


GELU_C = math.sqrt(2.0 / math.pi)
GELU_A = 0.044715


def _ffn_kernel(x_ref, xp_ref, xn_ref, p_ref, nwpre_ref, wup_ref, cw_ref, cb_ref, wdn_ref,
                nwpost_ref, wple_ref, wpg_ref, nwple_ref, out_ref, h_ref, slab_ref, act_ref,
                *, tm, tiles_per_seq):
    i = pl.program_id(0)
    pos = i % tiles_per_seq
    nw = nwpre_ref[...]
    x = x_ref[...]
    n = SUBLANES
    nch = tm // HG_CHUNK
    e = _dot(p_ref[...].astype(BF16), wple_ref[...])
    _to_slabs(slab_ref, _rms(x, nw))
    h_ref[0:tm, :] = _chunk_transposed(slab_ref, tm).astype(BF16)
    hp = _rms(xp_ref[...], nw) * jnp.where(pos == 0, 0.0, 1.0)
    hn = _rms(xn_ref[...], nw) * jnp.where(pos == tiles_per_seq - 1, 0.0, 1.0)
    h_ref[tm:, :] = jnp.concatenate([hp, hn], axis=0).astype(BF16)
    row = lax.broadcasted_iota(jnp.int32, (nch, n, FFN_TF), 1)

    def conv(u, col, scale):
        body = u[0:tm].reshape(nch, n, n, FFN_TF)
        before = u[tm:tm + n]
        after = u[tm + n:tm + 2 * n]
        first, last = body[:, 0], body[:, n - 1]
        prevs = jnp.concatenate([before[None], last[:-1]], axis=0)
        nexts = jnp.concatenate([first[1:], after[None]], axis=0)
        dn0 = jnp.where(row == 0, pltpu.roll(prevs, 1, axis=1), pltpu.roll(last, 1, axis=1))
        up7 = jnp.where(row == n - 1, pltpu.roll(nexts, n - 1, axis=1),
                        pltpu.roll(first, n - 1, axis=1))
        dn = jnp.concatenate([dn0[:, None], body[:, :n - 1]], axis=1)
        up = jnp.concatenate([body[:, 1:], up7[:, None]], axis=1)
        cw = cw_ref[:, col] * scale
        return dn * cw[0:1] + body * cw[1:2] + up * cw[2:3] + cb_ref[:, col] * scale

    for c in range(D_FF // FFN_TF):
        gcol = slice(c * FFN_TF, (c + 1) * FFN_TF)
        vcol = slice(D_FF + c * FFN_TF, D_FF + (c + 1) * FFN_TF)
        h = h_ref[...]
        g = conv(_dot(h, wup_ref[:, gcol]), gcol, 1.0)
        half_val = conv(_dot(h, wup_ref[:, vcol]), vcol, 0.5)
        t = jnp.tanh(g * (GELU_C + (GELU_C * GELU_A) * (g * g)))
        act_ref[:, gcol] = (g * (1.0 + t) * half_val).reshape(tm, FFN_TF).astype(BF16)

    nblk = 2 if nch % 2 == 0 else 1
    rb = tm // nblk
    blocks = [slice(b * rb, (b + 1) * rb) for b in range(nblk)]
    f = [_dot(act_ref[rows, :], wdn_ref[...]) for rows in blocks]
    x2, g = [], []
    for b, rows in enumerate(blocks):
        _to_slabs(slab_ref, f[b], b * rb)
        x2.append(x[rows] + _rms(_chunk_transposed(slab_ref, rb, b * rb), nwpost_ref[...]))
        g.append(_dot(_rms(x2[b]).astype(BF16), wpg_ref[...]))
    for b, rows in enumerate(blocks):
        out_ref[rows, :] = x2[b] + _rms(_sigmoid(g[b]) * e[rows], nwple_ref[...])


def _ffn_ple(x2d, p2d, nw_pre, w_up, conv_w, conv_b, w_down, nw_post,
             w_ple, w_ple_gate, nw_ple, layer, seq, tm):
    T = x2d.shape[0]
    tiles_per_seq = seq // tm
    hb = tm // FFN_HALO
    nhalo = T // FFN_HALO

    def lw(shape):
        return _const_spec((None,) + shape, lambda i: (layer,) + (0,) * len(shape))

    return pl.pallas_call(
        functools.partial(_ffn_kernel, tm=tm, tiles_per_seq=tiles_per_seq),
        grid=(T // tm,),
        in_specs=[
            pl.BlockSpec((tm, D_MODEL), lambda i: (i, 0)),
            pl.BlockSpec((FFN_HALO, D_MODEL), lambda i: (jnp.maximum(i * hb - 1, 0), 0)),
            pl.BlockSpec((FFN_HALO, D_MODEL), lambda i: (jnp.minimum((i + 1) * hb, nhalo - 1), 0)),
            pl.BlockSpec((None, tm, PLE_DIM), lambda i: (layer, i, 0)),
            lw((1, D_MODEL)),
            lw((D_MODEL, 2 * D_FF)),
            lw((3, 2 * D_FF)),
            lw((1, 2 * D_FF)),
            lw((D_FF, D_MODEL)),
            lw((1, D_MODEL)),
            lw((PLE_DIM, D_MODEL)),
            lw((D_MODEL, D_MODEL)),
            lw((1, D_MODEL)),
        ],
        out_specs=pl.BlockSpec((tm, D_MODEL), lambda i: (i, 0)),
        out_shape=jax.ShapeDtypeStruct((T, D_MODEL), F32),
        scratch_shapes=[
            pltpu.VMEM((tm + 2 * FFN_HALO, D_MODEL), BF16),
            pltpu.VMEM((D_MODEL // LANES, tm, LANES), F32),
            pltpu.VMEM((tm, D_FF), BF16),
        ],
        compiler_params=pltpu.CompilerParams(
            dimension_semantics=("parallel",), vmem_limit_bytes=VMEM_LIMIT),
        name="convffn_ple",
    )(x2d, x2d, x2d, p2d, nw_pre, w_up, conv_w, conv_b, w_down, nw_post,
      w_ple, w_ple_gate, nw_ple)


def _rope_tables(L):
    inv = 1.0 / (ROPE_THETA ** (jnp.arange(0, DA_DH, 2, dtype=F32) / DA_DH))
    ang = jnp.arange(L, dtype=F32)[:, None] * inv[None, :]
    cos = jnp.tile(jnp.cos(ang), (1, 4))
    sin = jnp.sin(ang)
    sin_signed = jnp.tile(jnp.concatenate([-sin, sin], axis=-1), (1, 2))
    return cos, sin_signed


def _trunk(x, p, w, tm_in, tm_mix, tm_ffn, tq, hg_heads=2):
    B, L, _ = x.shape
    T = B * L
    cos, sin = _rope_tables(L)
    x2d = x.reshape(T, D_MODEL)
    p3d = p.reshape(DEPTH, T, PLE_DIM)
    for l in range(DEPTH):
        (qh, vh, kf, kb, lgf, lgb, gg, q, k, v, gate) = _in_proj(
            x2d, w["norm_mix_pre"], w["w_in"], cos, sin, w["lb_logits"], l, L, tm_in)
        seq3 = lambda t: t.reshape(B, L, HG_WIDTH)
        a = _hgrn(seq3(qh), seq3(vh), seq3(kf), seq3(kb), seq3(lgf), seq3(lgb), seq3(gg),
                  w["hgrn_gnorm"], l, hg_heads)
        bb = _attention(q.reshape(B, L, DA_WIDTH), k.reshape(B, L, DA_WIDTH),
                        v.reshape(B, L, DA_WIDTH), w["diff_lambda"], w["diff_subln"], l, tq)
        x2d = _mix(x2d, a.reshape(T, HG_WIDTH), bb.reshape(T, DA_WIDTH), gate,
                   w["w_branch_a"], w["w_branch_b"], w["w_out"], w["norm_mix_post"], l, tm_mix)
        x2d = _ffn_ple(x2d, p3d, w["norm_ffn_pre"], w["w_up"], w["conv_w"], w["conv_b"],
                       w["w_down"], w["norm_ffn_post"], w["w_ple"], w["w_ple_gate"],
                       w["norm_ple"], l, L, tm_ffn)
    return x2d.reshape(B, L, D_MODEL)


def _prep_weights(w_in, hgrn_lb_logits, hgrn_gnorm, diff_lambda, diff_subln, w_branch_a,
                  w_branch_b, w_out, norm_mix_pre, norm_mix_post, w_up, conv_w, conv_b, w_down,
                  norm_ffn_pre, norm_ffn_post, w_ple, w_ple_gate, norm_ple):
    def vec(a):
        return a.reshape(DEPTH, 1, a.shape[-1]).astype(F32)

    return {
        "w_in": w_in.astype(BF16),
        "lb_logits": hgrn_lb_logits.reshape(2 * DEPTH, HG_WIDTH).astype(F32),
        "hgrn_gnorm": vec(hgrn_gnorm),
        "diff_lambda": diff_lambda.astype(F32),
        "diff_subln": vec(diff_subln),
        "w_branch_a": w_branch_a.astype(BF16),
        "w_branch_b": w_branch_b.astype(BF16),
        "w_out": w_out.astype(BF16),
        "norm_mix_pre": vec(norm_mix_pre),
        "norm_mix_post": vec(norm_mix_post),
        "w_up": w_up.astype(BF16),
        "conv_w": conv_w.astype(F32),
        "conv_b": vec(conv_b),
        "w_down": w_down.astype(BF16),
        "norm_ffn_pre": vec(norm_ffn_pre),
        "norm_ffn_post": vec(norm_ffn_post),
        "w_ple": w_ple.astype(BF16),
        "w_ple_gate": w_ple_gate.astype(BF16),
        "norm_ple": vec(norm_ple),
    }


def _tile(n, pref):
    t = min(n, pref)
    assert n % t == 0
    return t


def kernel(x_prompt, x_sample, p_prompt, p_sample, w_in, hgrn_lb_logits, hgrn_gnorm, diff_lambda, diff_subln, w_branch_a, w_branch_b, w_out, norm_mix_pre, norm_mix_post, w_up, conv_w, conv_b, w_down, norm_ffn_pre, norm_ffn_post, w_ple, w_ple_gate, norm_ple):
    w = _prep_weights(w_in, hgrn_lb_logits, hgrn_gnorm, diff_lambda, diff_subln, w_branch_a,
                      w_branch_b, w_out, norm_mix_pre, norm_mix_post, w_up, conv_w, conv_b,
                      w_down, norm_ffn_pre, norm_ffn_post, w_ple, w_ple_gate, norm_ple)
    outs = []
    for x, p in ((x_prompt, p_prompt), (x_sample, p_sample)):
        L = x.shape[1]
        hg_heads = HG_HEADS if L <= 2048 else HG_HEADS // 2
        outs.append(_trunk(x, p, w, tm_in=_tile(L, 512), tm_mix=_tile(L, 1024),
                           tm_ffn=_tile(L, 512), tq=_tile(L, 512), hg_heads=hg_heads))
    return tuple(outs)
```

```python
import functools
import math

import jax
import jax.numpy as jnp
from jax import lax
from jax.experimental import pallas as pl
from jax.experimental.pallas import tpu as pltpu

D_MODEL = 1024
DEPTH = 4
PLE_DIM = 256
HG_WIDTH = D_MODEL // 2
HG_HEADS = 4
HG_DK = HG_WIDTH // HG_HEADS
LB_FLOOR = 1e-30
DA_WIDTH = D_MODEL // 2
DA_HEADS = 4
DA_DV = DA_WIDTH // DA_HEADS
DA_DH = DA_DV // 2
D_FF = 2816
ROPE_THETA = 10000.0
EPS = 1e-6
IN_COLS = 5 * HG_WIDTH + 3 * DA_WIDTH + 2 * D_MODEL

LANES = 128
SUBLANES = 8
HG_CHUNK = SUBLANES * SUBLANES
HG_CHAINS = 32
VMEM_LIMIT = 56 * 1024 * 1024

BF16 = jnp.bfloat16
F32 = jnp.float32
LOG2E = math.log2(math.e)


def _dot(a, b):
    return jnp.dot(a, b, preferred_element_type=F32)


def _dot_nt(a, b):
    return lax.dot_general(a, b, (((1,), (1,)), ((), ())), preferred_element_type=F32)


def _dot_tn(a, b):
    return lax.dot_general(a, b, (((0,), (0,)), ((), ())), preferred_element_type=F32)


def _rms(x, w=None):
    y = x * lax.rsqrt(jnp.mean(x * x, axis=-1, keepdims=True) + EPS)
    return y if w is None else y * w


def _sigmoid(x):
    return 1.0 / (1.0 + jnp.exp(-x))


def _const_spec(block_shape, index_map):
    return pl.BlockSpec(block_shape, index_map, pipeline_mode=pl.Buffered(1))


def _to_slabs(slab_ref, val, row0=0):
    for s in range(val.shape[1] // LANES):
        slab_ref[s, row0:row0 + val.shape[0], :] = val[:, s * LANES:(s + 1) * LANES]


def _chunk_transposed(slab_ref, rows, row0=0):
    n = SUBLANES
    return jnp.concatenate(
        [jnp.concatenate([slab_ref[s, pl.ds(row0 + c * HG_CHUNK + j, n, stride=n), :]
                          for s in range(slab_ref.shape[0])], axis=1)
         for c in range(rows // HG_CHUNK) for j in range(n)], axis=0)


def _rope(z, cos, sin_signed, first_half):
    rot = jnp.where(first_half, pltpu.roll(z, 96, axis=1), pltpu.roll(z, 32, axis=1))
    return z * cos + rot * sin_signed


def _lower_bounds(lbl_ref, layer):
    out = []
    for d in range(2):
        rows = [lbl_ref[2 * l + d:2 * l + d + 1, :] for l in range(DEPTH)]
        mx = functools.reduce(jnp.maximum, rows)
        ex = [jnp.exp(r - mx) for r in rows]
        den = functools.reduce(lambda a_, b_: a_ + b_, ex)
        sm = [e / den for e in ex]
        cs = sm[0]
        for l in range(1, layer + 1):
            cs = cs + sm[l]
        out.append(cs - sm[0])
    return out


def _hgrn_gates(z, lb):
    t = jnp.exp(-jnp.abs(z))
    r = 1.0 / (1.0 + t)
    tr = t * r
    pos = z > 0
    sig_pos = jnp.where(pos, r, tr)
    sig_neg = jnp.where(pos, tr, r)
    one_m_lb = 1.0 - lb
    f = jnp.maximum(lb, LB_FLOOR) + one_m_lb * sig_pos
    return one_m_lb * sig_neg, jnp.log2(f)


def _in_proj_kernel(x_ref, nw_ref, w_ref, cos_ref, sin_ref, lbl_ref,
                    qh_ref, vh_ref, kf_ref, kb_ref, lgf_ref, lgb_ref, gg_ref,
                    q_ref, k_ref, v_ref, gate_ref, hn_ref, *, layer, tm):
    hn = _rms(x_ref[...], nw_ref[...])
    h = hn.astype(BF16)
    _to_slabs(hn_ref, hn)
    hp = _chunk_transposed(hn_ref, tm).astype(BF16)
    cw = HG_WIDTH
    lb = _lower_bounds(lbl_ref, layer)
    cos = cos_ref[...]
    sin = sin_ref[...]
    lane = lax.broadcasted_iota(jnp.int32, cos.shape, 1)
    first_half = (lane & (DA_DH - 1)) < (DA_DH // 2)
    base = 5 * HG_WIDTH
    gbase = base + 3 * DA_WIDTH

    def col(lhs, c):
        return _dot(lhs, w_ref[:, c * cw:(c + 1) * cw])

    def merge_gate(c):
        gate_ref[:, c * cw:(c + 1) * cw] = col(h, gbase // cw + c).astype(BF16)

    def rope_out(out_ref, z, scale):
        for hd in range(DA_HEADS):
            sl = slice(hd * LANES, (hd + 1) * LANES)
            out_ref[:, sl] = (_rope(z[:, sl], cos, sin, first_half) * scale).astype(BF16)

    zq_d = col(h, base // cw)
    zk_d = col(h, base // cw + 1)
    rope_out(q_ref, zq_d, LOG2E * DA_DH ** -0.5)
    rope_out(k_ref, zk_d, 1.0)
    zq = col(hp, 0)
    qh_ref[...] = (zq * _sigmoid(zq) * (HG_DK ** -0.5)).astype(BF16)
    kf, lgf = _hgrn_gates(col(hp, 1), lb[0])
    kf_ref[...] = kf.astype(BF16)
    lgf_ref[...] = lgf
    kb, lgb = _hgrn_gates(col(hp, 2), lb[1])
    kb_ref[...] = kb.astype(BF16)
    lgb_ref[...] = lgb
    vh_ref[...] = col(hp, 3).astype(BF16)
    zg = col(h, 4)
    gg_ref[...] = (zg * _sigmoid(zg)).astype(BF16)
    v_ref[...] = col(h, base // cw + 2).astype(BF16)
    for c in range(4):
        merge_gate(c)


def _in_proj(x2d, norm_w, w_in, cos, sin, lb_logits2d, layer, seq, tm):
    T = x2d.shape[0]
    npos = seq // tm
    half = lambda dt: (pl.BlockSpec((tm, HG_WIDTH), lambda i: (i, 0)),
                       jax.ShapeDtypeStruct((T, HG_WIDTH), dt))
    outs = [half(BF16), half(BF16), half(BF16), half(BF16), half(F32), half(F32), half(BF16),
            half(BF16), half(BF16), half(BF16),
            (pl.BlockSpec((tm, 2 * D_MODEL), lambda i: (i, 0)),
             jax.ShapeDtypeStruct((T, 2 * D_MODEL), BF16))]
    return pl.pallas_call(
        functools.partial(_in_proj_kernel, layer=layer, tm=tm),
        grid=(T // tm,),
        in_specs=[
            pl.BlockSpec((tm, D_MODEL), lambda i: (i, 0)),
            _const_spec((None, 1, D_MODEL), lambda i: (layer, 0, 0)),
            _const_spec((None, D_MODEL, IN_COLS), lambda i: (layer, 0, 0)),
            pl.BlockSpec((tm, LANES), lambda i: (i % npos, 0)),
            pl.BlockSpec((tm, LANES), lambda i: (i % npos, 0)),
            _const_spec((2 * DEPTH, HG_WIDTH), lambda i: (0, 0)),
        ],
        out_specs=[o[0] for o in outs],
        out_shape=[o[1] for o in outs],
        scratch_shapes=[pltpu.VMEM((D_MODEL // LANES, tm, LANES), F32)],
        compiler_params=pltpu.CompilerParams(
            dimension_semantics=("parallel",), vmem_limit_bytes=VMEM_LIMIT),
        name="in_proj",
    )(x2d, norm_w, w_in, cos, sin, lb_logits2d)


def _hgrn_consts():
    blk = jnp.arange(SUBLANES)[:, None]
    cblk = (jnp.arange(HG_CHUNK) % SUBLANES)[None, :]
    masks = []
    for rev in (False, True):
        for m in (1, 2, 4):
            same = (blk // (2 * m)) == (cblk // (2 * m))
            r_hi = (blk % (2 * m)) >= m
            c_hi = (cblk % (2 * m)) >= m
            mk = same & (r_hi & ~c_hi if not rev else ~r_hi & c_hi)
            masks.append(mk)
    return jnp.stack(masks).astype(F32)


def _gather_rows(x, src, row):
    out = None
    for s in sorted({(i - src[i]) % SUBLANES for i in range(SUBLANES)}):
        rolled = x if s == 0 else pltpu.roll(x, s, axis=0)
        if out is None:
            out = rolled
        else:
            sel = functools.reduce(
                jnp.logical_or,
                [row == i for i in range(SUBLANES) if (i - src[i]) % SUBLANES == s])
            out = jnp.where(sel, rolled, out)
    return out


def _pair_products(q, kk, w, rev):
    n = len(q)
    prods = {(j, j): q[j] * kk[j] for j in range(n)}

    def split(lo, hi):
        if hi - lo < 2:
            return
        mid = (lo + hi) // 2
        early, late = range(lo, mid), range(mid, hi)
        rows, cols, ref = (late, early, mid - 1) if not rev else (early, late, mid)
        if hi - lo == 2:
            (j,), (jp,) = rows, cols
            prods[(j, jp)] = q[j] * kk[jp] * jnp.exp2(w[j] - w[jp])
        else:
            qs = {j: q[j] * jnp.exp2(w[j] - w[ref]) for j in rows}
            ks = {jp: kk[jp] if jp == ref else kk[jp] * jnp.exp2(w[ref] - w[jp]) for jp in cols}
            for j in rows:
                for jp in cols:
                    prods[(j, jp)] = qs[j] * ks[jp]
        split(lo, mid)
        split(mid, hi)

    split(0, n)
    return prods


def _hgrn_prep(qb, kb, vb, lgc, rev, row, diag_sel):
    n = SUBLANES
    split = lambda a_: [a_[j * n:(j + 1) * n, :] for j in range(n)]
    q = split(qb.astype(F32))
    kk = split(kb.astype(F32))
    lg = split(lgc)
    order = list(range(n - 1, -1, -1)) if rev else list(range(n))
    w = [None] * n
    acc = None
    for j in order:
        acc = lg[j] if acc is None else acc + lg[j]
        w[j] = acc
    tot = acc
    incl = tot
    for s in (1, 2, 4):
        if not rev:
            incl = incl + jnp.where(row >= s, pltpu.roll(incl, s, axis=0), 0.0)
        else:
            incl = incl + jnp.where(row < n - s, pltpu.roll(incl, n - s, axis=0), 0.0)
    if not rev:
        excl = jnp.where(row >= 1, pltpu.roll(incl, 1, axis=0), 0.0)
        chunk_tot = jnp.broadcast_to(incl[n - 1:n, :], incl.shape)
    else:
        excl = jnp.where(row < n - 1, pltpu.roll(incl, n - 1, axis=0), 0.0)
        chunk_tot = jnp.broadcast_to(incl[0:1, :], incl.shape)
    b = [w[j] + excl for j in range(n)]

    bcat = jnp.concatenate(b, axis=0)

    c = {"vb": vb, "rev": rev}
    c["qe"] = qb * jnp.exp2(bcat).astype(BF16)

    c["lv"] = []
    for m in (1, 2, 4):
        if not rev:
            src = [(i // (2 * m)) * 2 * m + m - 1 for i in range(n)]
        else:
            src = [(i // (2 * m)) * 2 * m + m for i in range(n)]
        y = excl - _gather_rows(incl, src, row)
        e = jnp.concatenate([jnp.exp2(-jnp.abs(w[j] + y)) for j in range(n)], axis=0).astype(BF16)
        c["lv"].append((qb * e, kb * e))

    prods = _pair_products(q, kk, w, rev)
    rows_d = []
    for j in range(n):
        ad = None
        for jp in (range(j + 1) if not rev else range(j, n)):
            red = jnp.broadcast_to(jnp.sum(prods[(j, jp)], axis=-1, keepdims=True),
                                   (n, HG_CHUNK))
            ad = jnp.where(diag_sel[jp], red, 0.0 if ad is None else ad)
        rows_d.append(ad)
    c["ad"] = jnp.concatenate(rows_d, axis=0)

    tot_cat = jnp.concatenate([chunk_tot] * n, axis=0)
    c["kd"] = kb * jnp.exp2(tot_cat - bcat).astype(BF16)
    c["dec"] = jnp.exp2(chunk_tot[0:1, :])
    return c


def _hgrn_mm1(c, st_ref):
    st = st_ref[...]
    c["st"] = st
    c["o"] = _dot_nt(c["qe"], st.astype(BF16))
    c["am"] = [_dot_nt(qm, km) for qm, km in c["lv"]]
    c["kv"] = _dot_tn(c["vb"], c["kd"])


def _hgrn_mid(c, sel):
    n = SUBLANES
    pieces = []
    for j in range(n):
        rows = slice(j * n, (j + 1) * n)
        a = c["ad"][rows]
        for li, am in enumerate(c["am"]):
            a = jnp.where(sel[(3 if c["rev"] else 0) + li], am[rows], a)
        pieces.append(a)
    c["a"] = jnp.concatenate(pieces, axis=0).astype(BF16)


def _hgrn_mm2(c, st_ref):
    st_ref[...] = c["dec"] * c["st"] + c["kv"]
    return c["o"] + _dot(c["a"], c["vb"])


def _hgrn_kernel(qh_ref, vh_ref, kf_ref, kb_ref, lgf_ref, lgb_ref, gg_ref, gn_ref, masks_ref,
                 out_ref, o_ref, st_ref, *, seq, heads):
    n = SUBLANES
    nchunks = seq // HG_CHUNK
    unroll = max(1, HG_CHAINS // (2 * heads))
    while (nchunks // 2) % unroll:
        unroll //= 2
    row = lax.broadcasted_iota(jnp.int32, (n, HG_DK), 0)
    sel = [masks_ref[i] > 0.5 for i in range(masks_ref.shape[0])]
    col8 = lax.broadcasted_iota(jnp.int32, (n, HG_CHUNK), 1)
    row8 = lax.broadcasted_iota(jnp.int32, (n, HG_CHUNK), 0)
    diag_sel = [col8 == jp * n + row8 for jp in range(n)]
    st_ref[...] = jnp.zeros_like(st_ref)

    def make_body(accumulate):
        def body(it, carry):
            chains = [(hd, d) for hd in range(heads) for d in range(2)]
            ctx, bases = [], []
            for u in range(unroll):
                c = it * unroll + u
                fw = pl.multiple_of(c * HG_CHUNK, HG_CHUNK)
                bw = pl.multiple_of((nchunks - 1 - c) * HG_CHUNK, HG_CHUNK)
                bases.append((fw, bw))
                step = []
                for hd, d in chains:
                    rows = pl.ds(bw if d else fw, HG_CHUNK)
                    cols = slice(hd * HG_DK, (hd + 1) * HG_DK)
                    k_ref, lg_ref = (kb_ref, lgb_ref) if d else (kf_ref, lgf_ref)
                    step.append(_hgrn_prep(qh_ref[rows, cols], k_ref[rows, cols],
                                           vh_ref[rows, cols], lg_ref[rows, cols], d == 1, row,
                                           diag_sel))
                ctx.append(step)
            for u in range(unroll):
                for (hd, d), cx in zip(chains, ctx[u]):
                    _hgrn_mm1(cx, st_ref.at[2 * hd + d])
                for cx in ctx[u]:
                    _hgrn_mid(cx, sel)
                for (hd, d), cx in zip(chains, ctx[u]):
                    o = _hgrn_mm2(cx, st_ref.at[2 * hd + d])
                    base = bases[u][d]
                    for j in range(n):
                        dst = (hd, pl.ds(base + j, n, stride=n), slice(None))
                        piece = o[j * n:(j + 1) * n, :]
                        o_ref[dst] = o_ref[dst] + piece if accumulate else piece
            return carry
        return body

    half_iters = nchunks // (2 * unroll)
    lax.fori_loop(0, half_iters, make_body(False), 0)
    lax.fori_loop(half_iters, 2 * half_iters, make_body(True), 0)

    gn = gn_ref[...]
    rows_e = min(256, seq)

    def epilogue(c, carry):
        sl = pl.ds(pl.multiple_of(c * rows_e, rows_e), rows_e)
        for hd in range(heads):
            cols = slice(hd * HG_DK, (hd + 1) * HG_DK)
            y = _rms(o_ref[hd, sl, :], gn) * gg_ref[sl, cols].astype(F32)
            out_ref[sl, cols] = y.astype(out_ref.dtype)
        return carry

    lax.fori_loop(0, seq // rows_e, epilogue, 0)


def _hgrn(qh, vh, kf, kb, lgf, lgb, gg, gnorm, layer, heads):
    B, L, _ = qh.shape
    masks = _hgrn_consts()
    width = heads * HG_DK
    blk = pl.BlockSpec((None, L, width), lambda b, h: (b, 0, h))
    return pl.pallas_call(
        functools.partial(_hgrn_kernel, seq=L, heads=heads),
        grid=(B, HG_HEADS // heads),
        in_specs=[
            blk, blk, blk, blk, blk, blk, blk,
            _const_spec((None, 1, HG_DK), lambda b, h: (layer, 0, 0)),
            _const_spec(masks.shape, lambda b, h: (0, 0, 0)),
        ],
        out_specs=blk,
        out_shape=jax.ShapeDtypeStruct((B, L, HG_WIDTH), BF16),
        scratch_shapes=[
            pltpu.VMEM((heads, L, HG_DK), F32),
            pltpu.VMEM((2 * heads, HG_DK, HG_DK), F32),
        ],
        compiler_params=pltpu.CompilerParams(
            dimension_semantics=("parallel", "parallel"), vmem_limit_bytes=VMEM_LIMIT),
        name="hgrn2",
    )(qh, vh, kf, kb, lgf, lgb, gg, gnorm, masks)


ATTN_SUB = 256


ATTN_SLACK = 1.0 + 2.0 ** -6
ATTN_MIN_SUM = 2.0 ** -64


def _attn_combine(p, l, lam, v, sw, lambda_init):
    sub, seq = p[0].shape
    w = [jnp.broadcast_to(sc, (sub, LANES)).astype(BF16) for sc in (1.0 / l[0], lam / l[1])]
    a = jnp.concatenate(
        [p[0][:, t * LANES:(t + 1) * LANES] * w[0] - p[1][:, t * LANES:(t + 1) * LANES] * w[1]
         for t in range(seq // LANES)], axis=1)
    return _rms(_dot(a, v), sw) * (1.0 - lambda_init)


def _attn_kernel(q_ref, k_ref, v_ref, lam_ref, sw_ref, out_ref, kaug_ref, kmax_ref, vt_ref,
                 *, lambda_init):
    lp = lam_ref[...]
    lam = (jnp.exp(jnp.sum(lp[0:1] * lp[1:2], axis=-1, keepdims=True))
           - jnp.exp(jnp.sum(lp[2:3] * lp[3:4], axis=-1, keepdims=True)) + lambda_init)
    sw = sw_ref[...]
    tq = q_ref.shape[0]
    seq = k_ref.shape[0]
    sub = min(ATTN_SUB, tq)
    nsub = tq // sub
    spare_lane = (DA_DH, 0)

    def own_lanes(lane, i):
        return (lane < DA_DH) if i == 0 else (lane >= DA_DH)

    @pl.when(pl.program_id(2) == 0)
    def _():
        k = k_ref[...]
        k32 = k.astype(F32)
        sq = k32 * k32
        lane = lax.broadcasted_iota(jnp.int32, k.shape, 1)
        for i in range(2):
            own = own_lanes(lane, i)
            n2 = jnp.sum(jnp.where(own, sq, 0.0), axis=-1, keepdims=True)
            kmax_ref[i] = jnp.broadcast_to(jnp.sqrt(jnp.max(n2, axis=0, keepdims=True)),
                                           kmax_ref.shape[1:])
            kaug_ref[i] = jnp.where(
                own, k32, jnp.where(lane == spare_lane[i], -1.0, 0.0)).astype(BF16)
        vt_ref[...] = v_ref[...].astype(F32).T.astype(BF16)

    qa = []
    for r in range(nsub):
        q32 = q_ref[r * sub:(r + 1) * sub, :].astype(F32)
        sq = q32 * q32
        lane = lax.broadcasted_iota(jnp.int32, q32.shape, 1)
        pair = []
        for i in range(2):
            own = own_lanes(lane, i)
            qn = jnp.sqrt(jnp.sum(jnp.where(own, sq, 0.0), axis=-1, keepdims=True))
            m = qn * kmax_ref[i][0:1, 0:1] * ATTN_SLACK
            pair.append(
                jnp.where(own, q32, jnp.where(lane == spare_lane[i], m, 0.0)).astype(BF16))
        qa.append(pair)
    scores = [[_dot_nt(kaug_ref[i], qa[r][i]) for i in range(2)] for r in range(nsub)]
    lmin = None
    for r in range(nsub):
        p, l = [], []
        for i in range(2):
            e = jnp.exp2(scores[r][i])
            l.append(jnp.sum(e, axis=0, keepdims=True))
            p.append(e.astype(BF16))
            lmin = l[i] if lmin is None else jnp.minimum(lmin, l[i])
        a_t = p[0] * (1.0 / l[0]).astype(BF16) - p[1] * (lam / l[1]).astype(BF16)
        o = _dot(vt_ref[...], a_t).T
        out_ref[r * sub:(r + 1) * sub, :] = (
            _rms(o, sw) * (1.0 - lambda_init)).astype(out_ref.dtype)
    shifted_ok = jnp.min(lmin) >= ATTN_MIN_SUM

    @pl.when(jnp.logical_not(shifted_ok))
    def _():
        k = k_ref[...]
        for r in range(nsub):
            q = q_ref[r * sub:(r + 1) * sub, :]
            lane = lax.broadcasted_iota(jnp.int32, q.shape, 1)
            zero = jnp.zeros_like(q)
            p, l = [], []
            for i in range(2):
                s = _dot_nt(jnp.where(own_lanes(lane, i), q, zero), k)
                e = jnp.exp2(s - jnp.max(s, axis=-1, keepdims=True))
                l.append(jnp.sum(e, axis=-1, keepdims=True))
                p.append(e.astype(BF16))
            out_ref[r * sub:(r + 1) * sub, :] = _attn_combine(
                p, l, lam, v_ref[...], sw, lambda_init).astype(out_ref.dtype)


def _attention(q3d, k3d, v3d, lam_params, subln, layer, tq):
    B, L, _ = q3d.shape
    lambda_init = 0.8 - 0.6 * math.exp(-0.3 * layer)
    return pl.pallas_call(
        functools.partial(_attn_kernel, lambda_init=lambda_init),
        grid=(B, DA_HEADS, L // tq),
        in_specs=[
            pl.BlockSpec((None, tq, DA_DV), lambda b, h, i: (b, i, h)),
            pl.BlockSpec((None, L, DA_DV), lambda b, h, i: (b, 0, h)),
            pl.BlockSpec((None, L, DA_DV), lambda b, h, i: (b, 0, h)),
            _const_spec((None, 4, DA_DH), lambda b, h, i: (layer, 0, 0)),
            _const_spec((None, 1, DA_DV), lambda b, h, i: (layer, 0, 0)),
        ],
        out_specs=pl.BlockSpec((None, tq, DA_DV), lambda b, h, i: (b, i, h)),
        out_shape=jax.ShapeDtypeStruct((B, L, DA_WIDTH), BF16),
        scratch_shapes=[
            pltpu.VMEM((2, L, DA_DV), BF16),
            pltpu.VMEM((2, SUBLANES, LANES), F32),
            pltpu.VMEM((DA_DV, L), BF16),
        ],
        compiler_params=pltpu.CompilerParams(
            dimension_semantics=("parallel", "parallel", "arbitrary"),
            vmem_limit_bytes=VMEM_LIMIT),
        name="diff_attn",
    )(q3d, k3d, v3d, lam_params, subln)


def _mix_kernel(x_ref, a_ref, b_ref, gate_ref, wa_ref, wb_ref, wo_ref, nw_ref, out_ref):
    ga = _sigmoid(gate_ref[:, :D_MODEL].astype(F32))
    gb = _sigmoid(gate_ref[:, D_MODEL:].astype(F32))
    m = ga * _dot(a_ref[...], wa_ref[...]) + gb * _dot(b_ref[...], wb_ref[...])
    r = _dot(m.astype(BF16), wo_ref[...])
    out_ref[...] = x_ref[...] + _rms(r, nw_ref[...])


def _mix(x2d, a2d, b2d, gate2d, wa, wb, wo, norm_w, layer, tm):
    T = x2d.shape[0]
    return pl.pallas_call(
        _mix_kernel,
        grid=(T // tm,),
        in_specs=[
            pl.BlockSpec((tm, D_MODEL), lambda i: (i, 0)),
            pl.BlockSpec((tm, HG_WIDTH), lambda i: (i, 0)),
            pl.BlockSpec((tm, DA_WIDTH), lambda i: (i, 0)),
            pl.BlockSpec((tm, 2 * D_MODEL), lambda i: (i, 0)),
            _const_spec((None, HG_WIDTH, D_MODEL), lambda i: (layer, 0, 0)),
            _const_spec((None, DA_WIDTH, D_MODEL), lambda i: (layer, 0, 0)),
            _const_spec((None, D_MODEL, D_MODEL), lambda i: (layer, 0, 0)),
            _const_spec((None, 1, D_MODEL), lambda i: (layer, 0, 0)),
        ],
        out_specs=pl.BlockSpec((tm, D_MODEL), lambda i: (i, 0)),
        out_shape=jax.ShapeDtypeStruct((T, D_MODEL), F32),
        compiler_params=pltpu.CompilerParams(
            dimension_semantics=("parallel",), vmem_limit_bytes=VMEM_LIMIT),
        name="mix_out",
    )(x2d, a2d, b2d, gate2d, wa, wb, wo, norm_w)


FFN_HALO = 8
FFN_TF = 256


GELU_C = math.sqrt(2.0 / math.pi)
GELU_A = 0.044715


def _ffn_kernel(x_ref, xp_ref, xn_ref, p_ref, nwpre_ref, wup_ref, cw_ref, cb_ref, wdn_ref,
                nwpost_ref, wple_ref, wpg_ref, nwple_ref, out_ref, h_ref, slab_ref, act_ref,
                *, tm, tiles_per_seq):
    i = pl.program_id(0)
    pos = i % tiles_per_seq
    nw = nwpre_ref[...]
    x = x_ref[...]
    n = SUBLANES
    nch = tm // HG_CHUNK
    e = _dot(p_ref[...].astype(BF16), wple_ref[...])
    _to_slabs(slab_ref, _rms(x, nw))
    h_ref[0:tm, :] = _chunk_transposed(slab_ref, tm).astype(BF16)
    hp = _rms(xp_ref[...], nw) * jnp.where(pos == 0, 0.0, 1.0)
    hn = _rms(xn_ref[...], nw) * jnp.where(pos == tiles_per_seq - 1, 0.0, 1.0)
    h_ref[tm:, :] = jnp.concatenate([hp, hn], axis=0).astype(BF16)
    row = lax.broadcasted_iota(jnp.int32, (nch, n, FFN_TF), 1)

    def conv(u, col, scale):
        body = u[0:tm].reshape(nch, n, n, FFN_TF)
        before = u[tm:tm + n]
        after = u[tm + n:tm + 2 * n]
        first, last = body[:, 0], body[:, n - 1]
        prevs = jnp.concatenate([before[None], last[:-1]], axis=0)
        nexts = jnp.concatenate([first[1:], after[None]], axis=0)
        dn0 = jnp.where(row == 0, pltpu.roll(prevs, 1, axis=1), pltpu.roll(last, 1, axis=1))
        up7 = jnp.where(row == n - 1, pltpu.roll(nexts, n - 1, axis=1),
                        pltpu.roll(first, n - 1, axis=1))
        dn = jnp.concatenate([dn0[:, None], body[:, :n - 1]], axis=1)
        up = jnp.concatenate([body[:, 1:], up7[:, None]], axis=1)
        cw = cw_ref[:, col] * scale
        return dn * cw[0:1] + body * cw[1:2] + up * cw[2:3] + cb_ref[:, col] * scale

    for c in range(D_FF // FFN_TF):
        gcol = slice(c * FFN_TF, (c + 1) * FFN_TF)
        vcol = slice(D_FF + c * FFN_TF, D_FF + (c + 1) * FFN_TF)
        h = h_ref[...]
        g = conv(_dot(h, wup_ref[:, gcol]), gcol, 1.0)
        half_val = conv(_dot(h, wup_ref[:, vcol]), vcol, 0.5)
        t = jnp.tanh(g * (GELU_C + (GELU_C * GELU_A) * (g * g)))
        act_ref[:, gcol] = (g * (1.0 + t) * half_val).reshape(tm, FFN_TF).astype(BF16)

    nblk = 2 if nch % 2 == 0 else 1
    rb = tm // nblk
    blocks = [slice(b * rb, (b + 1) * rb) for b in range(nblk)]
    f = [_dot(act_ref[rows, :], wdn_ref[...]) for rows in blocks]
    x2, g = [], []
    for b, rows in enumerate(blocks):
        _to_slabs(slab_ref, f[b], b * rb)
        x2.append(x[rows] + _rms(_chunk_transposed(slab_ref, rb, b * rb), nwpost_ref[...]))
        g.append(_dot(_rms(x2[b]).astype(BF16), wpg_ref[...]))
    for b, rows in enumerate(blocks):
        out_ref[rows, :] = x2[b] + _rms(_sigmoid(g[b]) * e[rows], nwple_ref[...])


def _ffn_ple(x2d, p2d, nw_pre, w_up, conv_w, conv_b, w_down, nw_post,
             w_ple, w_ple_gate, nw_ple, layer, seq, tm):
    T = x2d.shape[0]
    tiles_per_seq = seq // tm
    hb = tm // FFN_HALO
    nhalo = T // FFN_HALO

    def lw(shape):
        return _const_spec((None,) + shape, lambda i: (layer,) + (0,) * len(shape))

    return pl.pallas_call(
        functools.partial(_ffn_kernel, tm=tm, tiles_per_seq=tiles_per_seq),
        grid=(T // tm,),
        in_specs=[
            pl.BlockSpec((tm, D_MODEL), lambda i: (i, 0)),
            pl.BlockSpec((FFN_HALO, D_MODEL), lambda i: (jnp.maximum(i * hb - 1, 0), 0)),
            pl.BlockSpec((FFN_HALO, D_MODEL), lambda i: (jnp.minimum((i + 1) * hb, nhalo - 1), 0)),
            pl.BlockSpec((None, tm, PLE_DIM), lambda i: (layer, i, 0)),
            lw((1, D_MODEL)),
            lw((D_MODEL, 2 * D_FF)),
            lw((3, 2 * D_FF)),
            lw((1, 2 * D_FF)),
            lw((D_FF, D_MODEL)),
            lw((1, D_MODEL)),
            lw((PLE_DIM, D_MODEL)),
            lw((D_MODEL, D_MODEL)),
            lw((1, D_MODEL)),
        ],
        out_specs=pl.BlockSpec((tm, D_MODEL), lambda i: (i, 0)),
        out_shape=jax.ShapeDtypeStruct((T, D_MODEL), F32),
        scratch_shapes=[
            pltpu.VMEM((tm + 2 * FFN_HALO, D_MODEL), BF16),
            pltpu.VMEM((D_MODEL // LANES, tm, LANES), F32),
            pltpu.VMEM((tm, D_FF), BF16),
        ],
        compiler_params=pltpu.CompilerParams(
            dimension_semantics=("parallel",), vmem_limit_bytes=VMEM_LIMIT),
        name="convffn_ple",
    )(x2d, x2d, x2d, p2d, nw_pre, w_up, conv_w, conv_b, w_down, nw_post,
      w_ple, w_ple_gate, nw_ple)


def _rope_tables(L):
    inv = 1.0 / (ROPE_THETA ** (jnp.arange(0, DA_DH, 2, dtype=F32) / DA_DH))
    ang = jnp.arange(L, dtype=F32)[:, None] * inv[None, :]
    cos = jnp.tile(jnp.cos(ang), (1, 4))
    sin = jnp.sin(ang)
    sin_signed = jnp.tile(jnp.concatenate([-sin, sin], axis=-1), (1, 2))
    return cos, sin_signed


def _trunk(x, p, w, tm_in, tm_mix, tm_ffn, tq, hg_heads=2):
    B, L, _ = x.shape
    T = B * L
    cos, sin = _rope_tables(L)
    x2d = x.reshape(T, D_MODEL)
    p3d = p.reshape(DEPTH, T, PLE_DIM)
    for l in range(DEPTH):
        (qh, vh, kf, kb, lgf, lgb, gg, q, k, v, gate) = _in_proj(
            x2d, w["norm_mix_pre"], w["w_in"], cos, sin, w["lb_logits"], l, L, tm_in)
        seq3 = lambda t: t.reshape(B, L, HG_WIDTH)
        a = _hgrn(seq3(qh), seq3(vh), seq3(kf), seq3(kb), seq3(lgf), seq3(lgb), seq3(gg),
                  w["hgrn_gnorm"], l, hg_heads)
        bb = _attention(q.reshape(B, L, DA_WIDTH), k.reshape(B, L, DA_WIDTH),
                        v.reshape(B, L, DA_WIDTH), w["diff_lambda"], w["diff_subln"], l, tq)
        x2d = _mix(x2d, a.reshape(T, HG_WIDTH), bb.reshape(T, DA_WIDTH), gate,
                   w["w_branch_a"], w["w_branch_b"], w["w_out"], w["norm_mix_post"], l, tm_mix)
        x2d = _ffn_ple(x2d, p3d, w["norm_ffn_pre"], w["w_up"], w["conv_w"], w["conv_b"],
                       w["w_down"], w["norm_ffn_post"], w["w_ple"], w["w_ple_gate"],
                       w["norm_ple"], l, L, tm_ffn)
    return x2d.reshape(B, L, D_MODEL)


def _prep_weights(w_in, hgrn_lb_logits, hgrn_gnorm, diff_lambda, diff_subln, w_branch_a,
                  w_branch_b, w_out, norm_mix_pre, norm_mix_post, w_up, conv_w, conv_b, w_down,
                  norm_ffn_pre, norm_ffn_post, w_ple, w_ple_gate, norm_ple):
    def vec(a):
        return a.reshape(DEPTH, 1, a.shape[-1]).astype(F32)

    return {
        "w_in": w_in.astype(BF16),
        "lb_logits": hgrn_lb_logits.reshape(2 * DEPTH, HG_WIDTH).astype(F32),
        "hgrn_gnorm": vec(hgrn_gnorm),
        "diff_lambda": diff_lambda.astype(F32),
        "diff_subln": vec(diff_subln),
        "w_branch_a": w_branch_a.astype(BF16),
        "w_branch_b": w_branch_b.astype(BF16),
        "w_out": w_out.astype(BF16),
        "norm_mix_pre": vec(norm_mix_pre),
        "norm_mix_post": vec(norm_mix_post),
        "w_up": w_up.astype(BF16),
        "conv_w": conv_w.astype(F32),
        "conv_b": vec(conv_b),
        "w_down": w_down.astype(BF16),
        "norm_ffn_pre": vec(norm_ffn_pre),
        "norm_ffn_post": vec(norm_ffn_post),
        "w_ple": w_ple.astype(BF16),
        "w_ple_gate": w_ple_gate.astype(BF16),
        "norm_ple": vec(norm_ple),
    }


def _tile(n, pref):
    t = min(n, pref)
    assert n % t == 0
    return t


def kernel(x_prompt, x_sample, p_prompt, p_sample, w_in, hgrn_lb_logits, hgrn_gnorm, diff_lambda, diff_subln, w_branch_a, w_branch_b, w_out, norm_mix_pre, norm_mix_post, w_up, conv_w, conv_b, w_down, norm_ffn_pre, norm_ffn_post, w_ple, w_ple_gate, norm_ple):
    w = _prep_weights(w_in, hgrn_lb_logits, hgrn_gnorm, diff_lambda, diff_subln, w_branch_a,
                      w_branch_b, w_out, norm_mix_pre, norm_mix_post, w_up, conv_w, conv_b,
                      w_down, norm_ffn_pre, norm_ffn_post, w_ple, w_ple_gate, norm_ple)
    outs = []
    for x, p in ((x_prompt, p_prompt), (x_sample, p_sample)):
        L = x.shape[1]
        hg_heads = HG_HEADS if L <= 2048 else HG_HEADS // 2
        outs.append(_trunk(x, p, w, tm_in=_tile(L, 512), tm_mix=_tile(L, 1024),
                           tm_ffn=_tile(L, 512), tq=_tile(L, 512), hg_heads=hg_heads))
    return tuple(outs)
```
